```python
import math
import jax
import jax.numpy as jnp
from jax import lax
import numpy as np

D_MODEL = 1024
BATCH = 8
SEQ = 2048
DEPTH = 2
DEC_BATCH = 128
DEC_SEQ = 4
PAST_LEN = 16384
PAGE_SIZE = 128

H_A = 4
DK_A = 32
DV_A = 64
GATE_RANK_A = 16
GATE_TAU_A = 16.0
H_B = 4
E_B = 64
DV_B = 64
H_C = 4
DK_C = 128
DV_C = 128
D_MIX = H_A * DV_A + H_B * DV_B + H_C * DV_C
D_FF = 2688
CHUNK = 64
EPS = 1e-6
M_INIT = -1e30
NEG_BIG = -1e30
SPLIT_SIZES = (
    H_A * DK_A, H_A * DK_A, H_A * DV_A, H_A * DV_A, GATE_RANK_A,
    H_B * E_B, H_B * E_B, H_B * DV_B, H_B * DV_B,
    H_C * DK_C, H_C * DK_C, H_C * DV_C, H_C * DV_C, H_C, H_C,
)
D_IN = sum(SPLIT_SIZES)

kernel_name = "hybrid_gla_hgrn2_mlstm_decoder_step"


def rmsnorm(x, g):
    xf = x.astype(jnp.float32)
    y = xf * lax.rsqrt(jnp.mean(xf * xf, axis=-1, keepdims=True) + EPS)
    return (y * g.astype(jnp.float32)).astype(x.dtype)


def swiglu(x, w_gate, w_up, w_down):
    return (jax.nn.silu(x @ w_gate) * (x @ w_up)) @ w_down


def to_chunks(x, c):
    b, t = x.shape[:2]
    return jnp.moveaxis(x.reshape((b, t // c, c) + x.shape[2:]), 1, 0)


def from_chunks(y):
    n, b, c = y.shape[:3]
    return jnp.moveaxis(y, 0, 1).reshape((b, n * c) + y.shape[3:])


def gated_linear_recurrence(q, k, v, log_a, s0):
    f32 = jnp.float32
    c = math.gcd(q.shape[1], CHUNK)
    causal = jnp.tril(jnp.ones((c, c), dtype=bool))[None, :, :, None, None]

    def step(s, inp):
        qc, kc, vc, gc = inp
        b = jnp.cumsum(gc, axis=1)
        rel = jnp.where(causal, b[:, :, None] - b[:, None, :], NEG_BIG)
        scores = jnp.einsum('bthk,bshk,btshk->bhts', qc, kc, jnp.exp(rel))
        o = (jnp.einsum('bhts,bshv->bthv', scores, vc)
             + jnp.einsum('bthk,bhkv->bthv', qc * jnp.exp(b), s))
        b_last = b[:, -1]
        s_new = (jnp.exp(b_last)[..., None] * s
                 + jnp.einsum('bshk,bshv->bhkv', kc * jnp.exp(b_last[:, None] - b), vc))
        return s_new, o

    xs = tuple(to_chunks(a.astype(f32), c) for a in (q, k, v, log_a))
    s_fin, o = lax.scan(step, s0.astype(f32), xs)
    return from_chunks(o), s_fin


def mlstm_recurrence(q, k, v, i_pre, f_pre, c0, n0, m0):
    f32 = jnp.float32
    c = math.gcd(q.shape[1], CHUNK)
    causal = jnp.tril(jnp.ones((c, c), dtype=bool))[None, :, :, None]

    def step(carry, inp):
        cm, n, m = carry
        qc, kc, vc, ic, fc = inp
        F = jnp.cumsum(jax.nn.log_sigmoid(fc), axis=1)
        log_d = jnp.where(causal, F[:, :, None] - F[:, None, :] + ic[:, None, :], NEG_BIG)
        log_prev = F + m[:, None]
        m_t = jnp.maximum(log_prev, jnp.max(log_d, axis=2))
        d = jnp.exp(log_d - m_t[:, :, None])
        w_prev = jnp.exp(log_prev - m_t)
        qk = jnp.einsum('bthk,bshk->btsh', qc, kc) * d
        num = (jnp.einsum('btsh,bshv->bthv', qk, vc)
               + w_prev[..., None] * jnp.einsum('bthk,bhkv->bthv', qc, cm))
        den = jnp.sum(qk, axis=2) + w_prev * jnp.einsum('bthk,bhk->bth', qc, n)
        h = num / jnp.maximum(jnp.abs(den), jnp.exp(-m_t))[..., None]
        m_new = m_t[:, -1]
        w_old = jnp.exp(F[:, -1] + m - m_new)
        w_s = jnp.exp(F[:, -1:] - F + ic - m_new[:, None])
        c_new = w_old[..., None, None] * cm + jnp.einsum('bsh,bshk,bshv->bhkv', w_s, kc, vc)
        n_new = w_old[..., None] * n + jnp.einsum('bsh,bshk->bhk', w_s, kc)
        return (c_new, n_new, m_new), h

    xs = tuple(to_chunks(a.astype(f32), c) for a in (q, k, v, i_pre, f_pre))
    carry0 = (c0.astype(f32), n0.astype(f32), m0.astype(f32))
    (c_fin, n_fin, m_fin), h = lax.scan(step, carry0, xs)
    return from_chunks(h), (c_fin, n_fin, m_fin)


def token_mixer(h, state, w_in, gla_w_lr, gla_b_lr, gla_norm, hgrn_lb, hgrn_norm,
                mlstm_i_bias, mlstm_f_bias, mlstm_norm, w_out):
    bsz, t, _ = h.shape
    f32 = jnp.float32
    split_at = np.cumsum(SPLIT_SIZES)[:-1].tolist()
    (a_q, a_k, a_v, a_g, a_lr, b_q, b_f, b_i, b_g,
     c_q, c_k, c_v, c_o, c_i, c_f) = jnp.split(h @ w_in, split_at, axis=-1)
    heads = lambda u, n: u.reshape(bsz, t, n, -1)
    s_gla, s_hgrn, c_mem, n_vec, m_vec = state

    log_alpha = jax.nn.log_sigmoid((a_lr @ gla_w_lr + gla_b_lr).astype(f32)) / GATE_TAU_A
    o_a, s_gla_new = gated_linear_recurrence(
        heads(a_q, H_A) * DK_A ** -0.5, heads(a_k, H_A), heads(a_v, H_A), heads(log_alpha, H_A), s_gla)
    o_a = rmsnorm(o_a, gla_norm) * jax.nn.silu(heads(a_g, H_A).astype(f32))

    lb = hgrn_lb.reshape(H_B, E_B)
    z = heads(b_f, H_B).astype(f32)
    k_b = (1.0 - lb) * jax.nn.sigmoid(-z)
    log_f = jnp.log1p(-k_b)
    o_b, s_hgrn_new = gated_linear_recurrence(
        jax.nn.silu(heads(b_q, H_B)), k_b, heads(b_i, H_B), log_f, s_hgrn)
    o_b = rmsnorm(o_b, hgrn_norm) * jax.nn.silu(heads(b_g, H_B).astype(f32))

    h_c, (c_new, n_new, m_new) = mlstm_recurrence(
        heads(c_q, H_C), heads(c_k, H_C) * DK_C ** -0.5, heads(c_v, H_C),
        c_i + mlstm_i_bias, c_f + mlstm_f_bias, c_mem, n_vec, m_vec)
    o_c = jax.nn.sigmoid(heads(c_o, H_C).astype(f32)) * rmsnorm(h_c, mlstm_norm)

    mixed = jnp.concatenate(
        [o_a.reshape(bsz, t, -1), o_b.reshape(bsz, t, -1), o_c.reshape(bsz, t, -1)], axis=-1).astype(h.dtype)
    return mixed @ w_out, (s_gla_new, s_hgrn_new, c_new, n_new, m_new)


def decoder_layer(x, state, norm_gains, ffn1_w_gate, ffn1_w_up, ffn1_w_down, w_in, gla_w_lr, gla_b_lr,
                  gla_norm, hgrn_lb, hgrn_norm, mlstm_i_bias, mlstm_f_bias, mlstm_norm, w_out,
                  ffn2_w_gate, ffn2_w_up, ffn2_w_down):
    g = norm_gains
    x = x + 0.5 * rmsnorm(swiglu(rmsnorm(x, g[0]), ffn1_w_gate, ffn1_w_up, ffn1_w_down), g[1])
    mix, new_state = token_mixer(rmsnorm(x, g[2]), state, w_in, gla_w_lr, gla_b_lr, gla_norm, hgrn_lb,
                                 hgrn_norm, mlstm_i_bias, mlstm_f_bias, mlstm_norm, w_out)
    x = x + rmsnorm(mix, g[3])
    x = x + 0.5 * rmsnorm(swiglu(rmsnorm(x, g[4]), ffn2_w_gate, ffn2_w_up, ffn2_w_down), g[5])
    return x, new_state


def setup_inputs(seed: int = 0) -> dict:
    key = jax.random.key(seed)
    ks = jax.random.split(key, 24)
    f32 = jnp.float32
    nrm = lambda k, shape, scale: scale * jax.random.normal(k, shape, f32)
    return {
        'x_prompt': nrm(ks[0], (BATCH, SEQ, D_MODEL), 1.0),
        'x_sample': nrm(ks[1], (DEC_BATCH, DEC_SEQ, D_MODEL), 1.0),
        'state_gla': nrm(ks[2], (DEPTH, DEC_BATCH, H_A, DK_A, DV_A), 1.0),
        'state_hgrn': nrm(ks[3], (DEPTH, DEC_BATCH, H_B, E_B, DV_B), 1.0),
        'state_mlstm_C': nrm(ks[4], (DEPTH, DEC_BATCH, H_C, DK_C, DV_C), 0.5),
        'state_mlstm_n': nrm(ks[5], (DEPTH, DEC_BATCH, H_C, DK_C), 0.5),
        'state_mlstm_m': 1.0 + nrm(ks[6], (DEPTH, DEC_BATCH, H_C), 0.5),
        'norm_gains': 1.0 + nrm(ks[7], (DEPTH, 6, D_MODEL), 0.05),
        'ffn1_w_gate': nrm(ks[8], (DEPTH, D_MODEL, D_FF), D_MODEL ** -0.5),
        'ffn1_w_up': nrm(ks[9], (DEPTH, D_MODEL, D_FF), D_MODEL ** -0.5),
        'ffn1_w_down': nrm(ks[10], (DEPTH, D_FF, D_MODEL), D_FF ** -0.5),
        'w_in': nrm(ks[11], (DEPTH, D_MODEL, D_IN), D_MODEL ** -0.5),
        'gla_w_lr': nrm(ks[12], (DEPTH, GATE_RANK_A, H_A * DK_A), GATE_RANK_A ** -0.5),
        'gla_b_lr': nrm(ks[13], (DEPTH, H_A * DK_A), 0.1),
        'gla_norm': 1.0 + nrm(ks[14], (DEPTH, DV_A), 0.05),
        'hgrn_lb_logits': nrm(ks[15], (DEPTH, H_B * E_B), 1.0),
        'hgrn_norm': 1.0 + nrm(ks[16], (DEPTH, DV_B), 0.05),
        'mlstm_i_bias': nrm(ks[17], (DEPTH, H_C), 0.1),
        'mlstm_f_bias': jnp.linspace(3.0, 6.0, H_C, dtype=f32)[None, :] + nrm(ks[18], (DEPTH, H_C), 0.1),
        'mlstm_norm': 1.0 + nrm(ks[19], (DEPTH, DV_C), 0.05),
        'w_out': nrm(ks[20], (DEPTH, D_MIX, D_MODEL), D_MIX ** -0.5),
        'ffn2_w_gate': nrm(ks[21], (DEPTH, D_MODEL, D_FF), D_MODEL ** -0.5),
        'ffn2_w_up': nrm(ks[22], (DEPTH, D_MODEL, D_FF), D_MODEL ** -0.5),
        'ffn2_w_down': nrm(ks[23], (DEPTH, D_FF, D_MODEL), D_FF ** -0.5),
    }


def reference(x_prompt, x_sample, state_gla, state_hgrn, state_mlstm_C, state_mlstm_n, state_mlstm_m,
              norm_gains, ffn1_w_gate, ffn1_w_up, ffn1_w_down, w_in, gla_w_lr, gla_b_lr, gla_norm,
              hgrn_lb_logits, hgrn_norm, mlstm_i_bias, mlstm_f_bias, mlstm_norm, w_out,
              ffn2_w_gate, ffn2_w_up, ffn2_w_down):
    f32 = jnp.float32
    lb_sm = jax.nn.softmax(hgrn_lb_logits.astype(f32), axis=0)
    lb_all = jnp.cumsum(lb_sm, axis=0) - lb_sm[0:1]
    zero_state = (
        jnp.zeros((BATCH, H_A, DK_A, DV_A), f32),
        jnp.zeros((BATCH, H_B, E_B, DV_B), f32),
        jnp.zeros((BATCH, H_C, DK_C, DV_C), f32),
        jnp.zeros((BATCH, H_C, DK_C), f32),
        jnp.full((BATCH, H_C), M_INIT, f32),
    )
    y_prompt, y_sample = x_prompt, x_sample
    prompt_states, sample_states = [], []
    for l in range(DEPTH):
        weights = (norm_gains[l], ffn1_w_gate[l], ffn1_w_up[l], ffn1_w_down[l], w_in[l], gla_w_lr[l],
                   gla_b_lr[l], gla_norm[l], lb_all[l], hgrn_norm[l], mlstm_i_bias[l], mlstm_f_bias[l],
                   mlstm_norm[l], w_out[l], ffn2_w_gate[l], ffn2_w_up[l], ffn2_w_down[l])
        y_prompt, st_p = decoder_layer(y_prompt, zero_state, *weights)
        past = (state_gla[l], state_hgrn[l], state_mlstm_C[l], state_mlstm_n[l], state_mlstm_m[l])
        y_sample, st_s = decoder_layer(y_sample, past, *weights)
        prompt_states.append(st_p)
        sample_states.append(st_s)
    gla_p, hgrn_p, c_p, n_p, m_p = (jnp.stack(s) for s in zip(*prompt_states))
    gla_s, hgrn_s, c_s, n_s, m_s = (jnp.stack(s) for s in zip(*sample_states))
    return (y_prompt, y_sample, gla_p, hgrn_p, c_p, n_p, m_p, gla_s, hgrn_s, c_s, n_s, m_s)
```

```python
import functools

import jax
import jax.numpy as jnp
import numpy as np
from jax import lax
from jax.experimental import pallas as pl
from jax.experimental.pallas import tpu as pltpu

F32 = jnp.float32
BF16 = jnp.bfloat16

D_MODEL = 1024
D_FF = 2688
DEPTH = 2
H = 4
DK_A, DV_A, RANK_A, TAU_A = 32, 64, 16, 16.0
E_B, DV_B = 64, 64
DK_C, DV_C = 128, 128
EPS = 1e-6
NEG_BIG = -1e30
M_INIT = -1e30
SPLIT_SIZES = (
    H * DK_A, H * DK_A, H * DV_A, H * DV_A, RANK_A,
    H * E_B, H * E_B, H * DV_B, H * DV_B,
    H * DK_C, H * DK_C, H * DV_C, H * DV_C, H, H,
)

LANES = 128
SUBLANES = 8
MXU_TILE = 256
VMEM_BYTES = 64 * 2**20

SUB = SUBLANES
LA, VA = H * DK_A, H * DV_A
LB, VB = H * E_B, H * DV_B
N_FF_TILES = pl.cdiv(D_FF, MXU_TILE)
D_FF_PAD = N_FF_TILES * MXU_TILE
TOKEN_TILE = 512

T_AQK, T_AV, T_AG, T_BQ, T_BF, T_BI, T_BG = 0, 1, 2, 3, 4, 5, 6
T_CQ, T_CK, T_CV, T_CO, T_SMALL = 7, 9, 11, 13, 15
N_IN_TILES = 16
LANE_I, LANE_F = RANK_A, RANK_A + H


def _rms(x, g):
    return x * lax.rsqrt(jnp.mean(x * x, axis=-1, keepdims=True) + EPS) * g


def _sigmoid(x):
    return 1.0 / (1.0 + jnp.exp(-x))


def _silu(x):
    return x * _sigmoid(x)


def _log_sigmoid(x):
    return jnp.minimum(x, 0.0) - jnp.log1p(jnp.exp(-jnp.abs(x)))


def _dot(a, b):
    return jnp.dot(a, b, preferred_element_type=F32)


def _dot_tn(a, b):
    return lax.dot_general(a, b, (((0,), (0,)), ((), ())), preferred_element_type=F32)


def _dot_nt(a, b):
    return lax.dot_general(a, b, (((1,), (1,)), ((), ())), preferred_element_type=F32)


def _tri_cumsum(tri16, x):
    hi = x.astype(BF16)
    r1 = x - hi.astype(F32)
    mid = r1.astype(BF16)
    lo = (r1 - mid.astype(F32)).astype(BF16)
    return _dot(tri16, hi) + _dot(tri16, mid) + _dot(tri16, lo)


def _ffn_kernel(x_ref, gpre_ref, gpost_ref, wg_ref, wu_ref, wd_ref, o_ref, h_scr, acc_scr):
    h_scr[...] = _rms(x_ref[...], gpre_ref[...]).astype(BF16)
    acc_scr[...] = jnp.zeros_like(acc_scr)

    def ff_tile(j, carry):
        h = h_scr[...]
        g = _dot(h, wg_ref[j])
        u = _dot(h, wu_ref[j])
        acc_scr[...] += _dot((_silu(g) * u).astype(BF16), wd_ref[j])
        return carry

    lax.fori_loop(0, N_FF_TILES, ff_tile, 0)
    o_ref[...] = x_ref[...] + 0.5 * _rms(acc_scr[...], gpost_ref[...])


def _const_spec(shape):
    return pl.BlockSpec(shape, lambda *_: (0,) * len(shape), pipeline_mode=pl.Buffered(1))


def _ffn(x, g_pre, g_post, wg, wu, wd):
    n = x.shape[0]
    tm = min(TOKEN_TILE, n)
    weights = 3 * D_MODEL * D_FF_PAD * 2
    tiles = tm * D_MODEL * (2 * 2 * 4 + 2 + 4) + 4 * tm * MXU_TILE * 4
    return pl.pallas_call(
        _ffn_kernel,
        name="swiglu_half_step",
        out_shape=jax.ShapeDtypeStruct((n, D_MODEL), F32),
        grid=(n // tm,),
        in_specs=[
            pl.BlockSpec((tm, D_MODEL), lambda i: (i, 0)),
            _const_spec((1, D_MODEL)),
            _const_spec((1, D_MODEL)),
            _const_spec((N_FF_TILES, D_MODEL, MXU_TILE)),
            _const_spec((N_FF_TILES, D_MODEL, MXU_TILE)),
            _const_spec((N_FF_TILES, MXU_TILE, D_MODEL)),
        ],
        out_specs=pl.BlockSpec((tm, D_MODEL), lambda i: (i, 0)),
        scratch_shapes=[pltpu.VMEM((tm, D_MODEL), BF16), pltpu.VMEM((tm, D_MODEL), F32)],
        compiler_params=pltpu.CompilerParams(
            dimension_semantics=("arbitrary",),
            vmem_limit_bytes=min(VMEM_BYTES, 2 * (weights + tiles)),
        ),
    )(x, g_pre, g_post, wg, wu, wd)


def _gated_linear(q, k, v, la, s_prev, pat32, pat16, tri_sub16):
    r, l = q.shape
    vv = v.shape[1]
    n_sub = r // SUB
    b = _tri_cumsum(tri_sub16, la)
    b3, q3, k3, v3 = (a.reshape(n_sub, SUB, a.shape[1]) for a in (b, q, k, v))
    bend3 = b3[:, SUB - 1:SUB, :]
    qh3 = q3 * jnp.exp(b3)
    kh3 = k3 * jnp.exp(bend3 - b3)

    tpos = lax.broadcasted_iota(jnp.int32, (n_sub, SUB, l), 1)
    parts = []
    for j in range(SUB):
        arg = jnp.where(tpos >= j, b3 - b3[:, j:j + 1, :], NEG_BIG)
        parts.append((q3 * k3[:, j:j + 1, :] * jnp.exp(arg)).reshape(r, l))
    pcat = jnp.concatenate(parts, axis=0).astype(BF16)
    sc = _dot(pcat, pat16)
    intra3 = jnp.zeros((n_sub, SUB, vv), F32)
    for j in range(SUB):
        intra3 = intra3 + sc[j * r:(j + 1) * r].reshape(n_sub, SUB, vv) * v3[:, j:j + 1, :]

    bend = bend3.reshape(n_sub, l)
    if n_sub < SUBLANES:
        bend = jnp.concatenate([bend, jnp.zeros((SUBLANES - n_sub, l), F32)], axis=0)
    dcols = jnp.exp(bend).T
    s = s_prev
    inter = []
    for i in range(n_sub):
        inter.append(_dot(qh3[i].astype(BF16), s.astype(BF16)))
        u = _dot_tn(kh3[i].astype(BF16), v3[i].astype(BF16))
        s = s * dcols[:, i:i + 1] + u * pat32
    o = inter[0] if n_sub == 1 else jnp.concatenate(inter, axis=0)
    return o + intra3.reshape(r, vv), s


def _expand_heads(s_c, pat32):
    return jnp.concatenate([s_c] * H, axis=1) * pat32


def _fold_heads(s, dv):
    return sum(s[:, h * dv:(h + 1) * dv] for h in range(H))


def _mixer_kernel(layer, nb, r, valid, has_init, n_steps, *refs):
    (x_ref, gpre_ref, gpost_ref, win_ref, wlr_ref, blr_ref, gnorm_ref, lb_ref, hnorm_ref,
     sbias_ref, mnorm_ref, wout_ref, pa32_ref, pa16_ref, pb32_ref, pb16_ref, trisub_ref,
     trifull_ref) = refs[:18]
    refs = refs[18:]
    if has_init:
        gla0_ref, hgrn0_ref, c0_ref, n0_ref, m0_ref = refs[:5]
        refs = refs[5:]
    y_ref, gla_ref, hgrn_ref, c_ref, n_ref, m_ref = refs[:6]
    hin_scr, proj_scr, mixed_scr = refs[6:9]
    if not has_init:
        sgla_scr, shgrn_scr = refs[9:]
    rows_all = nb * r
    step = pl.program_id(0)

    if not has_init:
        @pl.when(step == 0)
        def _():
            sgla_scr[...] = jnp.zeros_like(sgla_scr)
            shgrn_scr[...] = jnp.zeros_like(shgrn_scr)
            c_ref[...] = jnp.zeros_like(c_ref)
            n_ref[...] = jnp.zeros_like(n_ref)
            m_ref[...] = jnp.full(m_ref.shape, M_INIT, F32)

    hin_scr[...] = _rms(x_ref[...].reshape(rows_all, D_MODEL), gpre_ref[...]).astype(BF16)

    def in_tile(t, carry):
        proj_scr[t] = _dot(hin_scr[...], win_ref[t])
        return carry

    lax.fori_loop(0, N_IN_TILES, in_tile, 0)

    lg = [lb_ref[j:j + 1, :] for j in range(DEPTH)]
    lg_max = functools.reduce(jnp.maximum, lg)
    ex = [jnp.exp(v - lg_max) for v in lg]
    ex_sum = functools.reduce(jnp.add, ex)
    lb = functools.reduce(jnp.add, [e / ex_sum for e in ex[1:layer + 1]], jnp.zeros_like(lg_max))

    if valid < r:
        row_ok = lax.broadcasted_iota(jnp.int32, (r, 1), 0) < valid
    causal = (lax.broadcasted_iota(jnp.int32, (r, r), 0) >= lax.broadcasted_iota(jnp.int32, (r, r), 1))
    lane = lax.broadcasted_iota(jnp.int32, (r, LANES), 1)
    is_f_lane = (lane >= LANE_F) & (lane < LANE_F + H)

    def one_batch(bi, carry):
        rows = pl.ds(pl.multiple_of(bi * r, r), r)
        small = proj_scr[T_SMALL, rows, 0:LANES]

        la = _log_sigmoid(_dot(small.astype(BF16), wlr_ref[...]) + blr_ref[...]) * (1.0 / TAU_A)
        if valid < r:
            la = jnp.where(row_ok, la, 0.0)
        s0 = _expand_heads(gla0_ref[bi], pa32_ref[...]) if has_init else sgla_scr[bi]
        o_a, s1 = _gated_linear(
            proj_scr[T_AQK, rows, 0:LA] * DK_A ** -0.5, proj_scr[T_AQK, rows, LA:2 * LA],
            proj_scr[T_AV, rows, :], la, s0, pa32_ref[...], pa16_ref[...], trisub_ref[...])
        mixed_scr[rows, 0:VA] = o_a
        if has_init:
            gla_ref[bi] = _fold_heads(s1, DV_A)
        else:
            sgla_scr[bi] = s1

        k_b = (1.0 - lb) * _sigmoid(-proj_scr[T_BF, rows, :])
        if valid < r:
            k_b = jnp.where(row_ok, k_b, 0.0)
        s0 = _expand_heads(hgrn0_ref[bi], pb32_ref[...]) if has_init else shgrn_scr[bi]
        o_b, s1 = _gated_linear(
            _silu(proj_scr[T_BQ, rows, :]), k_b, proj_scr[T_BI, rows, :], jnp.log1p(-k_b), s0,
            pb32_ref[...], pb16_ref[...], trisub_ref[...])
        mixed_scr[rows, VA:VA + VB] = o_b
        if has_init:
            hgrn_ref[bi] = _fold_heads(s1, DV_B)
        else:
            shgrn_scr[bi] = s1

        gates = small + sbias_ref[...]
        lsf = _log_sigmoid(gates)
        if valid < r:
            lsf = jnp.where(row_ok, lsf, 0.0)
            gates = jnp.where(row_ok, gates, NEG_BIG)
        f_cum = _tri_cumsum(trifull_ref[...], lsf)
        by_row = jnp.where(is_f_lane, f_cum, gates).T
        c_src, n_src, m_src = (c0_ref, n0_ref, m0_ref) if has_init else (c_ref, n_ref, m_ref)
        for h in range(H):
            f_col = f_cum[:, LANE_F + h:LANE_F + h + 1]
            i_col = gates[:, LANE_I + h:LANE_I + h + 1]
            f_row = by_row[LANE_F + h:LANE_F + h + 1, :]
            i_row = by_row[LANE_I + h:LANE_I + h + 1, :]
            m_prev = m_src[pl.ds(bi, 1), h:h + 1]
            log_d = jnp.where(causal, f_col - f_row + i_row, NEG_BIG)
            log_prev = f_col + m_prev
            m_t = jnp.maximum(log_prev, jnp.max(log_d, axis=1, keepdims=True))
            d = jnp.exp(log_d - m_t)
            w_prev = jnp.exp(log_prev - m_t)
            q = proj_scr[T_CQ + h // 2, rows, (h % 2) * DK_C:(h % 2 + 1) * DK_C]
            k = proj_scr[T_CK + h // 2, rows, (h % 2) * DK_C:(h % 2 + 1) * DK_C] * DK_C ** -0.5
            v = proj_scr[T_CV + h // 2, rows, (h % 2) * DV_C:(h % 2 + 1) * DV_C]
            q16, k16, v16 = q.astype(BF16), k.astype(BF16), v.astype(BF16)
            c_prev = c_src[bi, h]
            n_prev = n_src[bi, h:h + 1, :]
            qk = _dot_nt(q16, k16) * d
            num = _dot(qk.astype(BF16), v16) + w_prev * _dot(q16, c_prev.astype(BF16))
            den = jnp.sum(qk, axis=1, keepdims=True) + w_prev * jnp.sum(q * n_prev, axis=1, keepdims=True)
            mixed_scr[rows, VA + VB + h * DV_C:VA + VB + (h + 1) * DV_C] = (
                num / jnp.maximum(jnp.abs(den), jnp.exp(-m_t)))
            m_new = m_t[r - 1:r, :]
            f_last = f_col[r - 1:r, :]
            w_old = jnp.exp(f_last + m_prev - m_new)
            kw = k * jnp.exp(f_last - f_col + i_col - m_new)
            c_ref[bi, h] = w_old * c_prev + _dot_tn(kw.astype(BF16), v16)
            n_ref[bi, h:h + 1, :] = w_old * n_prev + jnp.sum(kw, axis=0, keepdims=True)
            m_ref[pl.ds(bi, 1), h:h + 1] = m_new
        return carry

    lax.fori_loop(0, nb, one_batch, 0)

    if not has_init:
        @pl.when(step == n_steps - 1)
        def _():
            for b in range(nb):
                gla_ref[b] = _fold_heads(sgla_scr[b], DV_A)
                hgrn_ref[b] = _fold_heads(shgrn_scr[b], DV_B)

    o_a = mixed_scr[:, 0:VA]
    ms = _dot((o_a * o_a).astype(BF16), pb16_ref[...]) * (1.0 / DV_A)
    mixed_scr[:, 0:VA] = o_a * lax.rsqrt(ms + EPS) * gnorm_ref[...] * _silu(proj_scr[T_AG])
    o_b = mixed_scr[:, VA:VA + VB]
    ms = _dot((o_b * o_b).astype(BF16), pb16_ref[...]) * (1.0 / DV_B)
    mixed_scr[:, VA:VA + VB] = o_b * lax.rsqrt(ms + EPS) * hnorm_ref[...] * _silu(proj_scr[T_BG])
    for h in range(H):
        cols = slice(VA + VB + h * DV_C, VA + VB + (h + 1) * DV_C)
        gate = _sigmoid(proj_scr[T_CO + h // 2, :, (h % 2) * DV_C:(h % 2 + 1) * DV_C])
        mixed_scr[:, cols] = gate * _rms(mixed_scr[:, cols], mnorm_ref[...])
    y = _dot(mixed_scr[...].astype(BF16), wout_ref[...])
    y_ref[...] = (x_ref[...].reshape(rows_all, D_MODEL) + _rms(y, gpost_ref[...])).reshape(nb, r, D_MODEL)


def _mixer(layer, x, w, consts, init=None, *, nb, r, valid):
    batch, tokens, _ = x.shape
    has_init = init is not None
    if has_init:
        assert tokens == r
        n_steps = batch // nb
        x_map = lambda i: (i, 0, 0)
        st_map = lambda nd: (lambda i: (i,) + (0,) * (nd - 1))
    else:
        assert batch == nb
        n_steps = tokens // r
        x_map = lambda i: (0, i, 0)
        st_map = lambda nd: (lambda i: (0,) * nd)
    rows_all = nb * r
    tri_sub, tri_full = consts["tri"][r]

    state_shapes = [(batch, LA, DV_A), (batch, LB, DV_B), (batch, H, DK_C, DV_C), (batch, H, DK_C), (batch, H)]
    state_specs = [pl.BlockSpec((nb,) + s[1:], st_map(len(s))) for s in state_shapes]
    in_arrays = [x, w["g_mix_pre"], w["g_mix_post"], w["w_in"], w["w_lr"], w["b_lr"], w["gla_norm"],
                 w["lb_logits"], w["hgrn_norm"], w["small_bias"], w["mlstm_norm"], w["w_out"],
                 consts["pa32"], consts["pa16"], consts["pb32"], consts["pb16"], tri_sub, tri_full]
    in_specs = [pl.BlockSpec((nb, r, D_MODEL), x_map)] + [_const_spec(a.shape) for a in in_arrays[1:]]
    if has_init:
        in_arrays += list(init)
        in_specs += state_specs
    scratch = [
        pltpu.VMEM((rows_all, D_MODEL), BF16),
        pltpu.VMEM((N_IN_TILES, rows_all, MXU_TILE), F32),
        pltpu.VMEM((rows_all, D_MODEL), F32),
    ]
    if not has_init:
        scratch += [pltpu.VMEM((nb, LA, VA), F32), pltpu.VMEM((nb, LB, VB), F32)]
    state_block = nb * 4 * (LA * LANES + LB * LANES + H * DK_C * DV_C + SUBLANES * LANES + LANES)
    vmem = (2 * (D_MODEL * N_IN_TILES * MXU_TILE + D_MODEL * D_MODEL)
            + rows_all * (D_MODEL * (2 + 4 + 4 * 4) + N_IN_TILES * MXU_TILE * 4)
            + (state_block * 4 if has_init else state_block * 2 + nb * 4 * (LA * VA + LB * VB)))
    outs = pl.pallas_call(
        functools.partial(_mixer_kernel, layer, nb, r, valid, has_init, n_steps),
        name="token_mixer_sample" if has_init else "token_mixer_prompt",
        out_shape=[jax.ShapeDtypeStruct(x.shape, F32)] + [jax.ShapeDtypeStruct(s, F32) for s in state_shapes],
        grid=(n_steps,),
        in_specs=in_specs,
        out_specs=[pl.BlockSpec((nb, r, D_MODEL), x_map)] + state_specs,
        scratch_shapes=scratch,
        compiler_params=pltpu.CompilerParams(
            dimension_semantics=("arbitrary",),
            vmem_limit_bytes=min(VMEM_BYTES, vmem + 16 * 2**20),
        ),
    )(*in_arrays)
    return outs[0], outs[1:]


def _head_pattern(rows_per_head, cols_per_head):
    rr = np.arange(H * rows_per_head)[:, None] // rows_per_head
    cc = np.arange(H * cols_per_head)[None, :] // cols_per_head
    return (rr == cc).astype(np.float32)


def _constants(chunk_lengths):
    pa, pb = _head_pattern(DK_A, DV_A), _head_pattern(E_B, DV_B)
    tri = {}
    for r in chunk_lengths:
        i, j = np.arange(r)[:, None], np.arange(r)[None, :]
        full = (j <= i)
        tri[r] = (jnp.asarray(full & (i // SUB == j // SUB), BF16), jnp.asarray(full, BF16))
    return {"pa32": jnp.asarray(pa), "pa16": jnp.asarray(pa, BF16),
            "pb32": jnp.asarray(pb), "pb16": jnp.asarray(pb, BF16), "tri": tri}


def _tile_cols(w):
    k, n = w.shape
    return jnp.transpose(w.reshape(k, n // MXU_TILE, MXU_TILE), (1, 0, 2)).astype(BF16)


def _ffn_weights(w_gate, w_up, w_down):
    pad = D_FF_PAD - D_FF
    return (_tile_cols(jnp.pad(w_gate, ((0, 0), (0, pad)))), _tile_cols(jnp.pad(w_up, ((0, 0), (0, pad)))),
            jnp.pad(w_down, ((0, pad), (0, 0))).reshape(N_FF_TILES, MXU_TILE, D_MODEL).astype(BF16))


def _mixer_weights(l, norm_gains, w_in, gla_w_lr, gla_b_lr, gla_norm, hgrn_lb_logits, hgrn_norm,
                   mlstm_i_bias, mlstm_f_bias, mlstm_norm, w_out):
    (a_q, a_k, a_v, a_g, a_lr, b_q, b_f, b_i, b_g, c_q, c_k, c_v, c_o, c_i, c_f) = jnp.split(
        w_in[l], np.cumsum(SPLIT_SIZES)[:-1].tolist(), axis=1)
    small_pad = jnp.zeros((D_MODEL, MXU_TILE - RANK_A - 2 * H), F32)
    packed = jnp.concatenate(
        [a_q, a_k, a_v, a_g, b_q, b_f, b_i, b_g, c_q, c_k, c_v, c_o, a_lr, c_i, c_f, small_pad], axis=1)
    row = lambda v: v.reshape(1, -1).astype(F32)
    return {
        "g_mix_pre": row(norm_gains[l, 2]), "g_mix_post": row(norm_gains[l, 3]),
        "w_in": _tile_cols(packed),
        "w_lr": jnp.pad(gla_w_lr[l], ((0, LANES - RANK_A), (0, 0))).astype(BF16),
        "b_lr": row(gla_b_lr[l]),
        "gla_norm": row(jnp.tile(gla_norm[l], H)),
        "lb_logits": hgrn_lb_logits.astype(F32),
        "hgrn_norm": row(jnp.tile(hgrn_norm[l], H)),
        "small_bias": row(jnp.concatenate(
            [jnp.zeros((RANK_A,), F32), mlstm_i_bias[l], mlstm_f_bias[l], jnp.zeros((LANES - RANK_A - 2 * H,), F32)])),
        "mlstm_norm": row(mlstm_norm[l]),
        "w_out": w_out[l].astype(BF16),
    }


PROMPT_CHUNK = 64
SAMPLE_ROWS = SUBLANES
SAMPLE_GROUP = 8


def kernel(x_prompt, x_sample, state_gla, state_hgrn, state_mlstm_C, state_mlstm_n, state_mlstm_m, norm_gains, ffn1_w_gate, ffn1_w_up, ffn1_w_down, w_in, gla_w_lr, gla_b_lr, gla_norm, hgrn_lb_logits, hgrn_norm, mlstm_i_bias, mlstm_f_bias, mlstm_norm, w_out, ffn2_w_gate, ffn2_w_up, ffn2_w_down):
    bp, tp, _ = x_prompt.shape
    bs, ts, _ = x_sample.shape
    consts = _constants((PROMPT_CHUNK, SAMPLE_ROWS))
    yp = x_prompt.reshape(bp * tp, D_MODEL)
    ys = x_sample.reshape(bs * ts, D_MODEL)
    prompt_states, sample_states = [], []
    for l in range(DEPTH):
        g = norm_gains[l].astype(F32)
        ffn1 = _ffn_weights(ffn1_w_gate[l], ffn1_w_up[l], ffn1_w_down[l])
        ffn2 = _ffn_weights(ffn2_w_gate[l], ffn2_w_up[l], ffn2_w_down[l])
        mix = _mixer_weights(l, norm_gains, w_in, gla_w_lr, gla_b_lr, gla_norm, hgrn_lb_logits, hgrn_norm,
                             mlstm_i_bias, mlstm_f_bias, mlstm_norm, w_out)
        init = (state_gla[l].reshape(bs, LA, DV_A), state_hgrn[l].reshape(bs, LB, DV_B),
                state_mlstm_C[l], state_mlstm_n[l], state_mlstm_m[l])

        yp = _ffn(yp, g[0:1], g[1:2], *ffn1)
        ys = _ffn(ys, g[0:1], g[1:2], *ffn1)
        yp3, st_p = _mixer(l, yp.reshape(bp, tp, D_MODEL), mix, consts, nb=bp, r=PROMPT_CHUNK, valid=PROMPT_CHUNK)
        ys_pad = jnp.pad(ys.reshape(bs, ts, D_MODEL), ((0, 0), (0, SAMPLE_ROWS - ts), (0, 0)))
        ys3, st_s = _mixer(l, ys_pad, mix, consts, init, nb=SAMPLE_GROUP, r=SAMPLE_ROWS, valid=ts)
        yp = _ffn(yp3.reshape(bp * tp, D_MODEL), g[4:5], g[5:6], *ffn2)
        ys = _ffn(ys3[:, :ts].reshape(bs * ts, D_MODEL), g[4:5], g[5:6], *ffn2)
        prompt_states.append(st_p)
        sample_states.append(st_s)

    def stack(states, batch):
        gla, hgrn, c, n, m = (jnp.stack(s) for s in zip(*states))
        return (gla.reshape(DEPTH, batch, H, DK_A, DV_A), hgrn.reshape(DEPTH, batch, H, E_B, DV_B), c, n, m)

    return ((yp.reshape(bp, tp, D_MODEL), ys.reshape(bs, ts, D_MODEL))
            + stack(prompt_states, bp) + stack(sample_states, bs))
```

```python
import functools

import jax
import jax.numpy as jnp
import numpy as np
from jax import lax
from jax.experimental import pallas as pl
from jax.experimental.pallas import tpu as pltpu

F32 = jnp.float32
BF16 = jnp.bfloat16

D_MODEL = 1024
D_FF = 2688
DEPTH = 2
H = 4
DK_A, DV_A, RANK_A, TAU_A = 32, 64, 16, 16.0
E_B, DV_B = 64, 64
DK_C, DV_C = 128, 128
EPS = 1e-6
NEG_BIG = -1e30
M_INIT = -1e30
SPLIT_SIZES = (
    H * DK_A, H * DK_A, H * DV_A, H * DV_A, RANK_A,
    H * E_B, H * E_B, H * DV_B, H * DV_B,
    H * DK_C, H * DK_C, H * DV_C, H * DV_C, H, H,
)

LANES = 128
SUBLANES = 8
MXU_TILE = 256
VMEM_BYTES = 64 * 2**20

SUB = SUBLANES
LA, VA = H * DK_A, H * DV_A
LB, VB = H * E_B, H * DV_B
N_FF_TILES = pl.cdiv(D_FF, MXU_TILE)
D_FF_PAD = N_FF_TILES * MXU_TILE
TOKEN_TILE = 512

C_AQ, C_AK, C_AV, C_AG = 0, LA, 2 * LA, 2 * LA + VA
C_BQ = C_AG + VA
C_BF, C_BI, C_BG = C_BQ + LB, C_BQ + 2 * LB, C_BQ + 3 * LB
C_CQ = C_BG + VB
C_CK, C_CV, C_CO = C_CQ + H * DK_C, C_CQ + 2 * H * DK_C, C_CQ + 2 * H * DK_C + H * DV_C
C_SMALL = C_CO + H * DV_C
N_IN_TILES = pl.cdiv(C_SMALL + LANES, MXU_TILE)
D_IN_PAD = N_IN_TILES * MXU_TILE
IN_TILES_PER_ITER = 4
LANE_I, LANE_F = RANK_A, RANK_A + H

CHUNK_A, CHUNK_B = 64, 32
PAIR_GROUP = 8
FAST_DECAY_LIMIT = 60.0


def _rms(x, g):
    return x * lax.rsqrt(jnp.mean(x * x, axis=-1, keepdims=True) + EPS) * g


def _sigmoid(x):
    return 1.0 / (1.0 + jnp.exp(-x))


def _silu(x):
    return x * _sigmoid(x)


def _log_sigmoid(x):
    return jnp.minimum(x, 0.0) - jnp.log1p(jnp.exp(-jnp.abs(x)))


def _dot(a, b):
    return jnp.dot(a, b, preferred_element_type=F32)


def _dot_tn(a, b):
    return lax.dot_general(a, b, (((0,), (0,)), ((), ())), preferred_element_type=F32)


def _dot_nt(a, b):
    return lax.dot_general(a, b, (((1,), (1,)), ((), ())), preferred_element_type=F32)


def _tri_cumsum(tri16, x):
    t = tri16.shape[0]
    hi = x.astype(BF16)
    r1 = x - hi.astype(F32)
    mid = r1.astype(BF16)
    lo = (r1 - mid.astype(F32)).astype(BF16)
    out = []
    for s in range(0, x.shape[0], t):
        out.append(_dot(tri16, hi[s:s + t]) + _dot(tri16, mid[s:s + t]) + _dot(tri16, lo[s:s + t]))
    return out[0] if len(out) == 1 else jnp.concatenate(out, axis=0)


def _pad_rows(x, mult):
    pad = -x.shape[0] % mult
    return x if pad == 0 else jnp.concatenate([x, jnp.zeros((pad, x.shape[1]), x.dtype)], axis=0)


def _ffn_kernel(x_ref, gpre_ref, gpost_ref, wg_ref, wu_ref, wd_ref, o_ref, h_scr, acc_scr):
    h_scr[...] = _rms(x_ref[...], gpre_ref[...]).astype(BF16)
    acc_scr[...] = jnp.zeros_like(acc_scr)

    def ff_tile(j, carry):
        cols = pl.ds(pl.multiple_of(j * MXU_TILE, MXU_TILE), MXU_TILE)
        h = h_scr[...]
        g = _dot(h, wg_ref[:, cols])
        u = _dot(h, wu_ref[:, cols])
        acc_scr[...] += _dot((_silu(g) * u).astype(BF16), wd_ref[cols, :])
        return carry

    lax.fori_loop(0, N_FF_TILES, ff_tile, 0)
    o_ref[...] = x_ref[...] + 0.5 * _rms(acc_scr[...], gpost_ref[...])


def _const_spec(shape):
    return pl.BlockSpec(shape, lambda *_: (0,) * len(shape), pipeline_mode=pl.Buffered(1))


def _ffn(x, g_pre, g_post, wg, wu, wd):
    n = x.shape[0]
    tm = min(TOKEN_TILE, n)
    weights = 3 * D_MODEL * D_FF_PAD * 2
    tiles = tm * D_MODEL * (2 * 2 * 4 + 2 + 4) + 4 * tm * MXU_TILE * 4
    return pl.pallas_call(
        _ffn_kernel,
        name="swiglu_half_step",
        out_shape=jax.ShapeDtypeStruct((n, D_MODEL), F32),
        grid=(n // tm,),
        in_specs=[
            pl.BlockSpec((tm, D_MODEL), lambda i: (i, 0)),
            _const_spec((1, D_MODEL)),
            _const_spec((1, D_MODEL)),
            _const_spec((D_MODEL, D_FF_PAD)),
            _const_spec((D_MODEL, D_FF_PAD)),
            _const_spec((D_FF_PAD, D_MODEL)),
        ],
        out_specs=pl.BlockSpec((tm, D_MODEL), lambda i: (i, 0)),
        scratch_shapes=[pltpu.VMEM((tm, D_MODEL), BF16), pltpu.VMEM((tm, D_MODEL), F32)],
        compiler_params=pltpu.CompilerParams(
            dimension_semantics=("arbitrary",),
            vmem_limit_bytes=min(VMEM_BYTES, 2 * (weights + tiles)),
        ),
    )(x, g_pre, g_post, wg, wu, wd)


def _gated_linear_exact(q, k, v, la, s_prev, pat32, pat16, tri_sub16):
    r, l = q.shape
    vv = v.shape[1]
    n_sub = r // SUB
    b = _tri_cumsum(tri_sub16, la)
    b3, q3, k3, v3 = (a.reshape(n_sub, SUB, a.shape[1]) for a in (b, q, k, v))
    bend3 = b3[:, SUB - 1:SUB, :]
    qh3 = q3 * jnp.exp(b3)
    kh3 = k3 * jnp.exp(bend3 - b3)

    tpos = lax.broadcasted_iota(jnp.int32, (n_sub, SUB, l), 1)
    intra3 = jnp.zeros((n_sub, SUB, vv), F32)
    for j in range(SUB):
        arg = jnp.where(tpos >= j, b3 - b3[:, j:j + 1, :], NEG_BIG)
        p = (q3 * k3[:, j:j + 1, :] * jnp.exp(arg)).reshape(r, l)
        sc = _dot(p.astype(BF16), pat16)
        intra3 = intra3 + sc.reshape(n_sub, SUB, vv) * v3[:, j:j + 1, :]

    dcols = jnp.exp(_pad_rows(bend3.reshape(n_sub, l), SUBLANES)).T
    s = s_prev
    inter = []
    for i in range(n_sub):
        inter.append(_dot(qh3[i].astype(BF16), s.astype(BF16)))
        u = _dot_tn(kh3[i].astype(BF16), v3[i].astype(BF16))
        s = s * dcols[:, i:i + 1] + u * pat32
    o = inter[0] if n_sub == 1 else jnp.concatenate(inter, axis=0)
    return o + intra3.reshape(r, vv), s


def _gated_linear_fast(q, k, v, bc, s_scr, o_store, pat32, nb, r, c):
    l, vv = q.shape[1], v.shape[1]
    n_ch = r // c
    n_tot = nb * n_ch
    bc3, q3, k3, v3 = (a.reshape(n_tot, c, a.shape[1]) for a in (bc, q, k, v))
    bmid = bc3[:, c // 2 - 1:c // 2, :]
    blast = bc3[:, c - 1:c, :]
    qt = (q3 * jnp.exp(bc3 - bmid)).astype(BF16)
    kt = (k3 * jnp.exp(bmid - bc3)).astype(BF16)
    qc = (q3 * jnp.exp(bc3)).astype(BF16)
    ke = (k3 * jnp.exp(blast - bc3)).astype(BF16)
    v16 = v3.astype(BF16)
    dcols = jnp.exp(_pad_rows(blast.reshape(n_tot, l), SUBLANES)).T

    head_l = lax.broadcasted_iota(jnp.int32, (c, l), 1) // (l // H)
    head_v = lax.broadcasted_iota(jnp.int32, (c, vv), 1) // (vv // H)
    causal = (lax.broadcasted_iota(jnp.int32, (H * c, c), 0) % c >= lax.broadcasted_iota(jnp.int32, (H * c, c), 1))
    zero16 = jnp.zeros((), BF16)

    idxs = range(n_tot)
    lhs = [jnp.concatenate([jnp.where(head_l == h, qt[i], zero16) for h in range(H)], axis=0) for i in idxs]
    sc = [_dot_nt(lhs[i], kt[i]) for i in idxs]
    sc = [jnp.where(causal, sc[i], 0.0).astype(BF16) for i in idxs]
    ov = [_dot(sc[i], v16[i]) for i in idxs]
    intra = [functools.reduce(jnp.add, [jnp.where(head_v == h, ov[i][h * c:(h + 1) * c], 0.0) for h in range(H)])
             for i in idxs]
    u = [_dot_tn(ke[i], v16[i]) for i in idxs]

    for ci in range(n_ch):
        ids = [b * n_ch + ci for b in range(nb)]
        s_old = [s_scr[b] for b in range(nb)]
        inter = [_dot(qc[i], s_old[b].astype(BF16)) for b, i in enumerate(ids)]
        for b, i in enumerate(ids):
            o_store(b, ci * c, c, intra[i] + inter[b])
            s_scr[b] = s_old[b] * dcols[:, i:i + 1] + u[i] * pat32


def _expand_heads(s_c, pat32):
    return jnp.concatenate([s_c] * H, axis=1) * pat32


def _fold_heads(s, dv):
    return functools.reduce(jnp.add, [s[:, h * dv:(h + 1) * dv] for h in range(H)])


def _mixer_kernel(layer, nb, r, valid, has_init, n_steps, *refs):
    (x_ref, gpre_ref, gpost_ref, win_ref, wlr_ref, blr_ref, gnorm_ref, lb_ref, hnorm_ref,
     sbias_ref, mnorm_ref, wout_ref, pa32_ref, pa16_ref, pb32_ref, pb16_ref, trisub_ref,
     tria_ref, trib_ref, tric_ref) = refs[:20]
    refs = refs[20:]
    if has_init:
        gla0_ref, hgrn0_ref, c0_ref, n0_ref, m0_ref = refs[:5]
        refs = refs[5:]
    y_ref, gla_ref, hgrn_ref, c_ref, n_ref, m_ref = refs[:6]
    hin_scr, proj_scr, mixed_scr, sgla_scr, shgrn_scr = refs[6:]
    rows_all = nb * r
    ca, cb = min(CHUNK_A, r), min(CHUNK_B, r)
    step = pl.program_id(0)

    if has_init:
        for b in range(nb):
            sgla_scr[b] = _expand_heads(gla0_ref[b], pa32_ref[...])
            shgrn_scr[b] = _expand_heads(hgrn0_ref[b], pb32_ref[...])
        c_ref[...] = c0_ref[...]
        n_ref[...] = n0_ref[...]
        m_ref[...] = m0_ref[...]
    else:
        @pl.when(step == 0)
        def _():
            sgla_scr[...] = jnp.zeros_like(sgla_scr)
            shgrn_scr[...] = jnp.zeros_like(shgrn_scr)
            c_ref[...] = jnp.zeros_like(c_ref)
            n_ref[...] = jnp.zeros_like(n_ref)
            m_ref[...] = jnp.full(m_ref.shape, M_INIT, F32)

    hin_scr[...] = _rms(x_ref[...].reshape(rows_all, D_MODEL), gpre_ref[...]).astype(BF16)

    def in_tiles(t, carry):
        for u in range(IN_TILES_PER_ITER):
            cols = pl.ds(pl.multiple_of((t * IN_TILES_PER_ITER + u) * MXU_TILE, MXU_TILE), MXU_TILE)
            proj_scr[:, cols] = _dot(hin_scr[...], win_ref[:, cols])
        return carry

    lax.fori_loop(0, N_IN_TILES // IN_TILES_PER_ITER, in_tiles, 0)

    lg = [lb_ref[j:j + 1, :] for j in range(DEPTH)]
    lg_max = functools.reduce(jnp.maximum, lg)
    ex = [jnp.exp(v - lg_max) for v in lg]
    ex_sum = functools.reduce(jnp.add, ex)
    lb = functools.reduce(jnp.add, [e / ex_sum for e in ex[1:layer + 1]], jnp.zeros_like(lg_max))

    if valid < r:
        row_ok_all = lax.broadcasted_iota(jnp.int32, (rows_all, 1), 0) % r < valid
        row_ok = row_ok_all[0:r]

    def store_cols(c0):
        def store(b, t0, n, val):
            mixed_scr[b * r + t0:b * r + t0 + n, c0:c0 + val.shape[1]] = val
        return store

    def exact_path(q_of, k_of, v_of, la_of, s_scr, pat32_ref, pat16_ref, c0):
        def one(bi, carry):
            rows = pl.ds(pl.multiple_of(bi * r, r), r)
            o, s1 = _gated_linear_exact(q_of(rows), k_of(rows), v_of(rows), la_of(rows), s_scr[bi],
                                        pat32_ref[...], pat16_ref[...], trisub_ref[...])
            mixed_scr[rows, c0:c0 + o.shape[1]] = o
            s_scr[bi] = s1
            return carry
        lax.fori_loop(0, nb, one, 0)

    def gated_linear(q_of, k_of, v_of, la_of, tri_ref, s_scr, pat32_ref, pat16_ref, c0, c):
        everything = slice(None)
        k_all = k_of(everything)
        bc = _tri_cumsum(tri_ref[...], la_of(everything, k_all))
        safe = jnp.max(-bc) <= FAST_DECAY_LIMIT

        @pl.when(safe)
        def _():
            _gated_linear_fast(q_of(everything), k_all, v_of(everything), bc, s_scr,
                               store_cols(c0), pat32_ref[...], nb, r, c)

        @pl.when(jnp.logical_not(safe))
        def _():
            exact_path(q_of, k_of, v_of, la_of, s_scr, pat32_ref, pat16_ref, c0)

    def la_a(rows, k=None):
        small = proj_scr[rows, C_SMALL:C_SMALL + LANES]
        la = _log_sigmoid(_dot(small.astype(BF16), wlr_ref[...]) + blr_ref[...]) * (1.0 / TAU_A)
        if valid < r:
            la = jnp.where(row_ok_all if rows == slice(None) else row_ok, la, 0.0)
        return la

    gated_linear(lambda rows: proj_scr[rows, C_AQ:C_AQ + LA] * DK_A ** -0.5,
                 lambda rows: proj_scr[rows, C_AK:C_AK + LA],
                 lambda rows: proj_scr[rows, C_AV:C_AV + VA],
                 la_a, tria_ref, sgla_scr, pa32_ref, pa16_ref, 0, ca)

    def k_b(rows):
        kb = (1.0 - lb) * _sigmoid(-proj_scr[rows, C_BF:C_BF + LB])
        if valid < r:
            kb = jnp.where(row_ok_all if rows == slice(None) else row_ok, kb, 0.0)
        return kb

    gated_linear(lambda rows: _silu(proj_scr[rows, C_BQ:C_BQ + LB]), k_b,
                 lambda rows: proj_scr[rows, C_BI:C_BI + VB],
                 lambda rows, k=None: jnp.log1p(-(k_b(rows) if k is None else k)),
                 trib_ref, shgrn_scr, pb32_ref, pb16_ref, VA, cb)

    small = proj_scr[:, C_SMALL:C_SMALL + LANES]
    gates = small + sbias_ref[...]
    lsf = _log_sigmoid(gates)
    if valid < r:
        lsf = jnp.where(row_ok_all, lsf, 0.0)
        gates = jnp.where(row_ok_all, gates, NEG_BIG)
    f_cum = _tri_cumsum(tric_ref[...], lsf)
    lane = lax.broadcasted_iota(jnp.int32, (rows_all, LANES), 1)
    gate_rows = jnp.where((lane >= LANE_F) & (lane < LANE_F + H), f_cum, gates)
    causal = (lax.broadcasted_iota(jnp.int32, (r, r), 0) >= lax.broadcasted_iota(jnp.int32, (r, r), 1))

    def head_cols(c0, b, h):
        return proj_scr[b * r:(b + 1) * r, c0 + h * DK_C:c0 + (h + 1) * DK_C]

    for b0 in range(0, nb, PAIR_GROUP // H):
        pairs = [(b, h) for b in range(b0, b0 + PAIR_GROUP // H) for h in range(H)]
        by_row = {b: gate_rows[b * r:(b + 1) * r].T for b in range(b0, b0 + PAIR_GROUP // H)}
        f_col = {(b, h): f_cum[b * r:(b + 1) * r, LANE_F + h:LANE_F + h + 1] for b, h in pairs}
        i_col = {(b, h): gates[b * r:(b + 1) * r, LANE_I + h:LANE_I + h + 1] for b, h in pairs}
        m_prev = {(b, h): m_ref[b:b + 1, h:h + 1] for b, h in pairs}
        a_mat = {(b, h): jnp.where(causal, by_row[b][LANE_I + h:LANE_I + h + 1, :]
                                   - by_row[b][LANE_F + h:LANE_F + h + 1, :], NEG_BIG) for b, h in pairs}
        g = {p: jnp.maximum(m_prev[p], jnp.max(a_mat[p], axis=1, keepdims=True)) for p in pairs}
        d = {p: jnp.exp(a_mat[p] - g[p]) for p in pairs}
        w_prev = {p: jnp.exp(m_prev[p] - g[p]) for p in pairs}
        m_t = {p: f_col[p] + g[p] for p in pairs}
        q = {(b, h): head_cols(C_CQ, b, h) for b, h in pairs}
        k = {(b, h): head_cols(C_CK, b, h) * DK_C ** -0.5 for b, h in pairs}
        q16 = {p: q[p].astype(BF16) for p in pairs}
        v16 = {(b, h): head_cols(C_CV, b, h).astype(BF16) for b, h in pairs}
        qk = {p: _dot_nt(q16[p], k[p].astype(BF16)) * d[p] for p in pairs}
        c_prev = {(b, h): c_ref[b, h] for b, h in pairs}
        n_prev = {(b, h): n_ref[b, h:h + 1, :] for b, h in pairs}
        num = {p: _dot(qk[p].astype(BF16), v16[p]) + w_prev[p] * _dot(q16[p], c_prev[p].astype(BF16))
               for p in pairs}
        den = {p: jnp.sum(qk[p], axis=1, keepdims=True)
               + w_prev[p] * jnp.sum(q[p] * n_prev[p], axis=1, keepdims=True) for p in pairs}
        for b, h in pairs:
            mixed_scr[b * r:(b + 1) * r, VA + VB + h * DV_C:VA + VB + (h + 1) * DV_C] = (
                num[b, h] / jnp.maximum(jnp.abs(den[b, h]), jnp.exp(-m_t[b, h])))
        g_last = {p: g[p][r - 1:r, :] for p in pairs}
        w_old = {p: jnp.exp(m_prev[p] - g_last[p]) for p in pairs}
        kw = {p: k[p] * jnp.exp(i_col[p] - f_col[p] - g_last[p]) for p in pairs}
        upd = {p: _dot_tn(kw[p].astype(BF16), v16[p]) for p in pairs}
        for b, h in pairs:
            c_ref[b, h] = w_old[b, h] * c_prev[b, h] + upd[b, h]
            n_ref[b, h:h + 1, :] = w_old[b, h] * n_prev[b, h] + jnp.sum(kw[b, h], axis=0, keepdims=True)
            m_ref[b:b + 1, h:h + 1] = m_t[b, h][r - 1:r, :]

    def fold_states():
        for b in range(nb):
            gla_ref[b] = _fold_heads(sgla_scr[b], DV_A)
            hgrn_ref[b] = _fold_heads(shgrn_scr[b], DV_B)

    if has_init:
        fold_states()
    else:
        pl.when(step == n_steps - 1)(fold_states)

    o_a = mixed_scr[:, 0:VA]
    ms = _dot((o_a * o_a).astype(BF16), pb16_ref[...]) * (1.0 / DV_A)
    mixed_scr[:, 0:VA] = o_a * lax.rsqrt(ms + EPS) * gnorm_ref[...] * _silu(proj_scr[:, C_AG:C_AG + VA])
    o_b = mixed_scr[:, VA:VA + VB]
    ms = _dot((o_b * o_b).astype(BF16), pb16_ref[...]) * (1.0 / DV_B)
    mixed_scr[:, VA:VA + VB] = o_b * lax.rsqrt(ms + EPS) * hnorm_ref[...] * _silu(proj_scr[:, C_BG:C_BG + VB])
    for h in range(H):
        cols = slice(VA + VB + h * DV_C, VA + VB + (h + 1) * DV_C)
        gate = _sigmoid(proj_scr[:, C_CO + h * DV_C:C_CO + (h + 1) * DV_C])
        mixed_scr[:, cols] = gate * _rms(mixed_scr[:, cols], mnorm_ref[...])
    y = _dot(mixed_scr[...].astype(BF16), wout_ref[...])
    y_ref[...] = (x_ref[...].reshape(rows_all, D_MODEL) + _rms(y, gpost_ref[...])).reshape(nb, r, D_MODEL)


def _block_tri(n, block):
    i, j = np.arange(n)[:, None], np.arange(n)[None, :]
    return jnp.asarray((j <= i) & (i // block == j // block), BF16)


def _mixer(layer, x, w, consts, init=None, *, nb, r, valid):
    batch, tokens, _ = x.shape
    has_init = init is not None
    if has_init:
        assert tokens == r
        n_steps = batch // nb
        x_map = lambda i: (i, 0, 0)
        st_map = lambda nd: (lambda i: (i,) + (0,) * (nd - 1))
    else:
        assert batch == nb
        n_steps = tokens // r
        x_map = lambda i: (0, i, 0)
        st_map = lambda nd: (lambda i: (0,) * nd)
    rows_all = nb * r
    slab = min(MXU_TILE, rows_all)
    tris = [_block_tri(r, SUB), _block_tri(slab, min(CHUNK_A, r)), _block_tri(slab, min(CHUNK_B, r)),
            _block_tri(slab, r)]

    state_shapes = [(batch, LA, DV_A), (batch, LB, DV_B), (batch, H, DK_C, DV_C), (batch, H, DK_C), (batch, H)]
    state_specs = [pl.BlockSpec((nb,) + s[1:], st_map(len(s))) for s in state_shapes]
    in_arrays = [x, w["g_mix_pre"], w["g_mix_post"], w["w_in"], w["w_lr"], w["b_lr"], w["gla_norm"],
                 w["lb_logits"], w["hgrn_norm"], w["small_bias"], w["mlstm_norm"], w["w_out"],
                 consts["pa32"], consts["pa16"], consts["pb32"], consts["pb16"]] + tris
    in_specs = [pl.BlockSpec((nb, r, D_MODEL), x_map)] + [_const_spec(a.shape) for a in in_arrays[1:]]
    if has_init:
        in_arrays += list(init)
        in_specs += state_specs
    scratch = [
        pltpu.VMEM((rows_all, D_MODEL), BF16),
        pltpu.VMEM((rows_all, D_IN_PAD), F32),
        pltpu.VMEM((rows_all, D_MODEL), F32),
        pltpu.VMEM((nb, LA, VA), F32),
        pltpu.VMEM((nb, LB, VB), F32),
    ]
    state_block = nb * 4 * (LA * LANES + LB * LANES + H * DK_C * DV_C + SUBLANES * LANES + LANES)
    vmem = (2 * (D_MODEL * D_IN_PAD + D_MODEL * D_MODEL)
            + rows_all * (D_MODEL * (2 + 4 + 4 * 4) + D_IN_PAD * 4)
            + nb * 4 * (LA * VA + LB * VB) + state_block * (4 if has_init else 2))
    outs = pl.pallas_call(
        functools.partial(_mixer_kernel, layer, nb, r, valid, has_init, n_steps),
        name="token_mixer_sample" if has_init else "token_mixer_prompt",
        out_shape=[jax.ShapeDtypeStruct(x.shape, F32)] + [jax.ShapeDtypeStruct(s, F32) for s in state_shapes],
        grid=(n_steps,),
        in_specs=in_specs,
        out_specs=[pl.BlockSpec((nb, r, D_MODEL), x_map)] + state_specs,
        scratch_shapes=scratch,
        compiler_params=pltpu.CompilerParams(
            dimension_semantics=("arbitrary",),
            vmem_limit_bytes=min(VMEM_BYTES, vmem + 16 * 2**20),
        ),
    )(*in_arrays)
    return outs[0], outs[1:]


def _head_pattern(rows_per_head, cols_per_head):
    rr = np.arange(H * rows_per_head)[:, None] // rows_per_head
    cc = np.arange(H * cols_per_head)[None, :] // cols_per_head
    return (rr == cc).astype(np.float32)


def _constants():
    pa, pb = _head_pattern(DK_A, DV_A), _head_pattern(E_B, DV_B)
    return {"pa32": jnp.asarray(pa), "pa16": jnp.asarray(pa, BF16),
            "pb32": jnp.asarray(pb), "pb16": jnp.asarray(pb, BF16)}


def _ffn_weights(w_gate, w_up, w_down):
    pad = D_FF_PAD - D_FF
    return (jnp.pad(w_gate, ((0, 0), (0, pad))).astype(BF16), jnp.pad(w_up, ((0, 0), (0, pad))).astype(BF16),
            jnp.pad(w_down, ((0, pad), (0, 0))).astype(BF16))


def _mixer_weights(l, norm_gains, w_in, gla_w_lr, gla_b_lr, gla_norm, hgrn_lb_logits, hgrn_norm,
                   mlstm_i_bias, mlstm_f_bias, mlstm_norm, w_out):
    (a_q, a_k, a_v, a_g, a_lr, b_q, b_f, b_i, b_g, c_q, c_k, c_v, c_o, c_i, c_f) = jnp.split(
        w_in[l], np.cumsum(SPLIT_SIZES)[:-1].tolist(), axis=1)
    small_pad = jnp.zeros((D_MODEL, D_IN_PAD - C_SMALL - RANK_A - 2 * H), F32)
    packed = jnp.concatenate(
        [a_q, a_k, a_v, a_g, b_q, b_f, b_i, b_g, c_q, c_k, c_v, c_o, a_lr, c_i, c_f, small_pad], axis=1)
    row = lambda v: v.reshape(1, -1).astype(F32)
    return {
        "g_mix_pre": row(norm_gains[l, 2]), "g_mix_post": row(norm_gains[l, 3]),
        "w_in": packed.astype(BF16),
        "w_lr": jnp.pad(gla_w_lr[l], ((0, LANES - RANK_A), (0, 0))).astype(BF16),
        "b_lr": row(gla_b_lr[l]),
        "gla_norm": row(jnp.tile(gla_norm[l], H)),
        "lb_logits": hgrn_lb_logits.astype(F32),
        "hgrn_norm": row(jnp.tile(hgrn_norm[l], H)),
        "small_bias": row(jnp.concatenate(
            [jnp.zeros((RANK_A,), F32), mlstm_i_bias[l], mlstm_f_bias[l], jnp.zeros((LANES - RANK_A - 2 * H,), F32)])),
        "mlstm_norm": row(mlstm_norm[l]),
        "w_out": w_out[l].astype(BF16),
    }


PROMPT_CHUNK = 64
SAMPLE_ROWS = SUBLANES
SAMPLE_GROUP = 8


def kernel(x_prompt, x_sample, state_gla, state_hgrn, state_mlstm_C, state_mlstm_n, state_mlstm_m, norm_gains, ffn1_w_gate, ffn1_w_up, ffn1_w_down, w_in, gla_w_lr, gla_b_lr, gla_norm, hgrn_lb_logits, hgrn_norm, mlstm_i_bias, mlstm_f_bias, mlstm_norm, w_out, ffn2_w_gate, ffn2_w_up, ffn2_w_down):
    bp, tp, _ = x_prompt.shape
    bs, ts, _ = x_sample.shape
    consts = _constants()
    yp = x_prompt.reshape(bp * tp, D_MODEL)
    ys = x_sample.reshape(bs * ts, D_MODEL)
    prompt_states, sample_states = [], []
    for l in range(DEPTH):
        g = norm_gains[l].astype(F32)
        ffn1 = _ffn_weights(ffn1_w_gate[l], ffn1_w_up[l], ffn1_w_down[l])
        ffn2 = _ffn_weights(ffn2_w_gate[l], ffn2_w_up[l], ffn2_w_down[l])
        mix = _mixer_weights(l, norm_gains, w_in, gla_w_lr, gla_b_lr, gla_norm, hgrn_lb_logits, hgrn_norm,
                             mlstm_i_bias, mlstm_f_bias, mlstm_norm, w_out)
        init = (state_gla[l].reshape(bs, LA, DV_A), state_hgrn[l].reshape(bs, LB, DV_B),
                state_mlstm_C[l], state_mlstm_n[l], state_mlstm_m[l])

        yp = _ffn(yp, g[0:1], g[1:2], *ffn1)
        ys = _ffn(ys, g[0:1], g[1:2], *ffn1)
        yp3, st_p = _mixer(l, yp.reshape(bp, tp, D_MODEL), mix, consts, nb=bp, r=PROMPT_CHUNK, valid=PROMPT_CHUNK)
        ys_pad = jnp.pad(ys.reshape(bs, ts, D_MODEL), ((0, 0), (0, SAMPLE_ROWS - ts), (0, 0)))
        ys3, st_s = _mixer(l, ys_pad, mix, consts, init, nb=SAMPLE_GROUP, r=SAMPLE_ROWS, valid=ts)
        yp = _ffn(yp3.reshape(bp * tp, D_MODEL), g[4:5], g[5:6], *ffn2)
        ys = _ffn(ys3[:, :ts].reshape(bs * ts, D_MODEL), g[4:5], g[5:6], *ffn2)
        prompt_states.append(st_p)
        sample_states.append(st_s)

    def stack(states, batch):
        gla, hgrn, c, n, m = (jnp.stack(s) for s in zip(*states))
        return (gla.reshape(DEPTH, batch, H, DK_A, DV_A), hgrn.reshape(DEPTH, batch, H, E_B, DV_B), c, n, m)

    return ((yp.reshape(bp, tp, D_MODEL), ys.reshape(bs, ts, D_MODEL))
            + stack(prompt_states, bp) + stack(sample_states, bs))
```

```python
import functools

import jax
import jax.numpy as jnp
import numpy as np
from jax import lax
from jax.experimental import pallas as pl
from jax.experimental.pallas import tpu as pltpu

F32 = jnp.float32
BF16 = jnp.bfloat16

D_MODEL = 1024
D_FF = 2688
DEPTH = 2
H = 4
DK_A, DV_A, RANK_A, TAU_A = 32, 64, 16, 16.0
E_B, DV_B = 64, 64
DK_C, DV_C = 128, 128
EPS = 1e-6
NEG_BIG = -1e30
M_INIT = -1e30
SPLIT_SIZES = (
    H * DK_A, H * DK_A, H * DV_A, H * DV_A, RANK_A,
    H * E_B, H * E_B, H * DV_B, H * DV_B,
    H * DK_C, H * DK_C, H * DV_C, H * DV_C, H, H,
)

LANES = 128
SUBLANES = 8
MXU_TILE = 256
VMEM_BYTES = 64 * 2**20

SUB = SUBLANES
LA, VA = H * DK_A, H * DV_A
LB, VB = H * E_B, H * DV_B
N_FF_TILES, FF_TAIL = divmod(D_FF, MXU_TILE)
TOKEN_TILE = 512

C_AQ, C_AK, C_AV, C_AG = 0, LA, 2 * LA, 2 * LA + VA
C_BQ = C_AG + VA
C_BF, C_BI, C_BG = C_BQ + LB, C_BQ + 2 * LB, C_BQ + 3 * LB
C_CQ = C_BG + VB
C_CK, C_CV, C_CO = C_CQ + H * DK_C, C_CQ + 2 * H * DK_C, C_CQ + 2 * H * DK_C + H * DV_C
C_SMALL = C_CO + H * DV_C
C_FGATE = C_SMALL + LANES
N_IN_TILES = pl.cdiv(C_FGATE + LANES, MXU_TILE)
D_IN_PAD = N_IN_TILES * MXU_TILE
IN_TILES_PER_ITER = 4
LANE_G = RANK_A

CHUNK_A, CHUNK_B = 64, 32
PAIR_GROUP = 8
FAST_DECAY_LIMIT = 60.0


def _rms(x, g):
    return x * lax.rsqrt(jnp.mean(x * x, axis=-1, keepdims=True) + EPS) * g


def _sigmoid(x):
    return 1.0 / (1.0 + jnp.exp(-x))


def _silu(x):
    return x * _sigmoid(x)


def _log_sigmoid(x):
    return jnp.minimum(x, 0.0) - jnp.log1p(jnp.exp(-jnp.abs(x)))


def _dot(a, b):
    return jnp.dot(a, b, preferred_element_type=F32)


def _dot_tn(a, b):
    return lax.dot_general(a, b, (((0,), (0,)), ((), ())), preferred_element_type=F32)


def _dot_nt(a, b):
    return lax.dot_general(a, b, (((1,), (1,)), ((), ())), preferred_element_type=F32)


def _tri_cumsum(tri16, x):
    t = tri16.shape[0]
    hi = x.astype(BF16)
    r1 = x - hi.astype(F32)
    mid = r1.astype(BF16)
    lo = (r1 - mid.astype(F32)).astype(BF16)
    out = []
    for s in range(0, x.shape[0], t):
        out.append(_dot(tri16, hi[s:s + t]) + _dot(tri16, mid[s:s + t]) + _dot(tri16, lo[s:s + t]))
    return out[0] if len(out) == 1 else jnp.concatenate(out, axis=0)


def _pad_rows(x, mult):
    pad = -x.shape[0] % mult
    return x if pad == 0 else jnp.concatenate([x, jnp.zeros((pad, x.shape[1]), x.dtype)], axis=0)


def _ffn_kernel(gain_row, x_ref, gains_ref, wg_ref, wu_ref, wd_ref, o_ref, h_scr, acc_scr):
    gpre_ref, gpost_ref = gains_ref.at[gain_row:gain_row + 1, :], gains_ref.at[gain_row + 1:gain_row + 2, :]
    h_scr[...] = _rms(x_ref[...], gpre_ref[...]).astype(BF16)
    acc_scr[...] = jnp.zeros_like(acc_scr)

    def ff_cols(cols):
        h = h_scr[...]
        g = _dot(h, wg_ref[:, cols])
        u = _dot(h, wu_ref[:, cols])
        acc_scr[...] += _dot((_silu(g) * u).astype(BF16), wd_ref[cols, :])

    def ff_tile(j, carry):
        ff_cols(pl.ds(pl.multiple_of(j * MXU_TILE, MXU_TILE), MXU_TILE))
        return carry

    lax.fori_loop(0, N_FF_TILES, ff_tile, 0)
    if FF_TAIL:
        ff_cols(pl.ds(N_FF_TILES * MXU_TILE, FF_TAIL))
    o_ref[...] = x_ref[...] + 0.5 * _rms(acc_scr[...], gpost_ref[...])


def _const_spec(shape):
    return pl.BlockSpec(shape, lambda *_: (0,) * len(shape), pipeline_mode=pl.Buffered(1))


def _layer_spec(layer, shape):
    return pl.BlockSpec((None,) + tuple(shape), lambda *_: (layer,) + (0,) * len(shape),
                        pipeline_mode=pl.Buffered(1))


def _ffn(layer, gain_row, x, gains, wg, wu, wd):
    n = x.shape[0]
    tm = min(TOKEN_TILE, n)
    weights = 3 * D_MODEL * D_FF * 2
    tiles = tm * D_MODEL * (2 * 2 * 4 + 2 + 4) + 4 * tm * MXU_TILE * 4
    return pl.pallas_call(
        functools.partial(_ffn_kernel, gain_row),
        name="swiglu_half_step",
        out_shape=jax.ShapeDtypeStruct((n, D_MODEL), F32),
        grid=(n // tm,),
        in_specs=[
            pl.BlockSpec((tm, D_MODEL), lambda i: (i, 0)),
            _layer_spec(layer, gains.shape[1:]),
            _layer_spec(layer, (D_MODEL, D_FF)),
            _layer_spec(layer, (D_MODEL, D_FF)),
            _layer_spec(layer, (D_FF, D_MODEL)),
        ],
        out_specs=pl.BlockSpec((tm, D_MODEL), lambda i: (i, 0)),
        scratch_shapes=[pltpu.VMEM((tm, D_MODEL), BF16), pltpu.VMEM((tm, D_MODEL), F32)],
        compiler_params=pltpu.CompilerParams(
            dimension_semantics=("arbitrary",),
            vmem_limit_bytes=min(VMEM_BYTES, 2 * (weights + tiles)),
        ),
    )(x, gains, wg, wu, wd)


def _gated_linear_exact(q, k, v, la, s_prev, pat32, pat16, tri_sub16):
    r, l = q.shape
    vv = v.shape[1]
    n_sub = r // SUB
    b = _tri_cumsum(tri_sub16, la)
    b3, q3, k3, v3 = (a.reshape(n_sub, SUB, a.shape[1]) for a in (b, q, k, v))
    bend3 = b3[:, SUB - 1:SUB, :]
    qh3 = q3 * jnp.exp(b3)
    kh3 = k3 * jnp.exp(bend3 - b3)

    tpos = lax.broadcasted_iota(jnp.int32, (n_sub, SUB, l), 1)
    intra3 = jnp.zeros((n_sub, SUB, vv), F32)
    for j in range(SUB):
        arg = jnp.where(tpos >= j, b3 - b3[:, j:j + 1, :], NEG_BIG)
        p = (q3 * k3[:, j:j + 1, :] * jnp.exp(arg)).reshape(r, l)
        sc = _dot(p.astype(BF16), pat16)
        intra3 = intra3 + sc.reshape(n_sub, SUB, vv) * v3[:, j:j + 1, :]

    dcols = jnp.exp(_pad_rows(bend3.reshape(n_sub, l), SUBLANES)).T
    s = s_prev
    inter = []
    for i in range(n_sub):
        inter.append(_dot(qh3[i].astype(BF16), s.astype(BF16)))
        u = _dot_tn(kh3[i].astype(BF16), v3[i].astype(BF16))
        s = s * dcols[:, i:i + 1] + u * pat32
    o = inter[0] if n_sub == 1 else jnp.concatenate(inter, axis=0)
    return o + intra3.reshape(r, vv), s


def _gated_linear_fast(q, k, v, bc, s_scr, o_store, pat32, nb, r, c):
    l, vv = q.shape[1], v.shape[1]
    n_ch = r // c
    n_tot = nb * n_ch
    bc3, q3, k3, v3 = (a.reshape(n_tot, c, a.shape[1]) for a in (bc, q, k, v))
    bmid = bc3[:, c // 2 - 1:c // 2, :]
    blast = bc3[:, c - 1:c, :]
    qt = (q3 * jnp.exp(bc3 - bmid)).astype(BF16)
    kt = (k3 * jnp.exp(bmid - bc3)).astype(BF16)
    qc = (q3 * jnp.exp(bc3)).astype(BF16)
    ke = (k3 * jnp.exp(blast - bc3)).astype(BF16)
    v16 = v3.astype(BF16)
    dcols = jnp.exp(_pad_rows(blast.reshape(n_tot, l), SUBLANES)).T

    head_l = lax.broadcasted_iota(jnp.int32, (c, l), 1) // (l // H)
    head_v = lax.broadcasted_iota(jnp.int32, (c, vv), 1) // (vv // H)
    causal = (lax.broadcasted_iota(jnp.int32, (H * c, c), 0) % c >= lax.broadcasted_iota(jnp.int32, (H * c, c), 1))
    zero16 = jnp.zeros((), BF16)

    idxs = range(n_tot)
    lhs = [jnp.concatenate([jnp.where(head_l == h, qt[i], zero16) for h in range(H)], axis=0) for i in idxs]
    sc = [_dot_nt(lhs[i], kt[i]) for i in idxs]
    sc = [jnp.where(causal, sc[i], 0.0).astype(BF16) for i in idxs]
    ov = [_dot(sc[i], v16[i]) for i in idxs]
    intra = [functools.reduce(jnp.add, [jnp.where(head_v == h, ov[i][h * c:(h + 1) * c], 0.0) for h in range(H)])
             for i in idxs]
    u = [_dot_tn(ke[i], v16[i]) for i in idxs]

    for ci in range(n_ch):
        ids = [b * n_ch + ci for b in range(nb)]
        s_old = [s_scr[b] for b in range(nb)]
        inter = [_dot(qc[i], s_old[b].astype(BF16)) for b, i in enumerate(ids)]
        for b, i in enumerate(ids):
            o_store(b, ci * c, c, intra[i] + inter[b])
            s_scr[b] = s_old[b] * dcols[:, i:i + 1] + u[i] * pat32


def _expand_heads(s_c, pat32):
    return jnp.concatenate([s_c] * H, axis=1) * pat32


def _fold_heads(s, dv):
    return functools.reduce(jnp.add, [s[:, h * dv:(h + 1) * dv] for h in range(H)])


def _mixer_kernel(layer, nb, r, valid, has_init, n_steps, *refs):
    (x_ref, gains_ref, win_ref, wlr_ref, blr_ref, gnorm_ref, lb_ref, hnorm_ref,
     ibias_ref, fbias_ref, mnorm_ref, wout_ref, pa32_ref, pa16_ref, pb32_ref, pb16_ref, trisub_ref,
     tria_ref, trib_ref, tric_ref) = refs[:20]
    refs = refs[20:]
    gpre_ref, gpost_ref = gains_ref.at[2:3, :], gains_ref.at[3:4, :]
    if has_init:
        gla0_ref, hgrn0_ref, c0_ref, n0_ref, m0_ref = refs[:5]
        refs = refs[5:]
    y_ref, gla_ref, hgrn_ref, c_ref, n_ref, m_ref = refs[:6]
    hin_scr, proj_scr, mixed_scr, sgla_scr, shgrn_scr, caug_scr, mrep_scr = refs[6:]
    rows_all = nb * r
    ca, cb = min(CHUNK_A, r), min(CHUNK_B, r)
    step = pl.program_id(0)

    if has_init:
        for b in range(nb):
            sgla_scr[b] = _expand_heads(gla0_ref[b], pa32_ref[...])
            shgrn_scr[b] = _expand_heads(hgrn0_ref[b], pb32_ref[...])
            for h in range(H):
                n_col = jnp.broadcast_to(n0_ref[b, h:h + 1, :], (DK_C, DK_C)).T
                caug_scr[b, h] = jnp.concatenate([c0_ref[b, h], n_col], axis=1)
                mrep_scr[b, h:h + 1, :] = jnp.broadcast_to(m0_ref[b:b + 1, h:h + 1], (1, LANES))
    else:
        @pl.when(step == 0)
        def _():
            sgla_scr[...] = jnp.zeros_like(sgla_scr)
            shgrn_scr[...] = jnp.zeros_like(shgrn_scr)
            caug_scr[...] = jnp.zeros_like(caug_scr)
            mrep_scr[...] = jnp.full(mrep_scr.shape, M_INIT, F32)

    hin_scr[...] = _rms(x_ref[...].reshape(rows_all, D_MODEL), gpre_ref[...]).astype(BF16)

    def in_tiles(t, carry):
        for u in range(IN_TILES_PER_ITER):
            cols = pl.ds(pl.multiple_of((t * IN_TILES_PER_ITER + u) * MXU_TILE, MXU_TILE), MXU_TILE)
            proj_scr[:, cols] = _dot(hin_scr[...], win_ref[:, cols])
        return carry

    lax.fori_loop(0, N_IN_TILES // IN_TILES_PER_ITER, in_tiles, 0)

    lg = [lb_ref[j:j + 1, :] for j in range(DEPTH)]
    lg_max = functools.reduce(jnp.maximum, lg)
    ex = [jnp.exp(v - lg_max) for v in lg]
    ex_sum = functools.reduce(jnp.add, ex)
    lb = functools.reduce(jnp.add, [e / ex_sum for e in ex[1:layer + 1]], jnp.zeros_like(lg_max))

    if valid < r:
        row_ok_all = lax.broadcasted_iota(jnp.int32, (rows_all, 1), 0) % r < valid
        row_ok = row_ok_all[0:r]

    def store_cols(c0):
        def store(b, t0, n, val):
            mixed_scr[b * r + t0:b * r + t0 + n, c0:c0 + val.shape[1]] = val
        return store

    def exact_path(q_of, k_of, v_of, la_of, s_scr, pat32_ref, pat16_ref, c0):
        def one(bi, carry):
            rows = pl.ds(pl.multiple_of(bi * r, r), r)
            o, s1 = _gated_linear_exact(q_of(rows), k_of(rows), v_of(rows), la_of(rows), s_scr[bi],
                                        pat32_ref[...], pat16_ref[...], trisub_ref[...])
            mixed_scr[rows, c0:c0 + o.shape[1]] = o
            s_scr[bi] = s1
            return carry
        lax.fori_loop(0, nb, one, 0)

    def gated_linear(q_of, k_of, v_of, la_of, tri_ref, s_scr, pat32_ref, pat16_ref, c0, c):
        everything = slice(None)
        k_all = k_of(everything)
        bc = _tri_cumsum(tri_ref[...], la_of(everything, k_all))
        safe = jnp.max(-bc) <= FAST_DECAY_LIMIT

        @pl.when(safe)
        def _():
            _gated_linear_fast(q_of(everything), k_all, v_of(everything), bc, s_scr,
                               store_cols(c0), pat32_ref[...], nb, r, c)

        @pl.when(jnp.logical_not(safe))
        def _():
            exact_path(q_of, k_of, v_of, la_of, s_scr, pat32_ref, pat16_ref, c0)

    def la_a(rows, k=None):
        small = proj_scr[rows, C_SMALL:C_SMALL + LANES]
        la = _log_sigmoid(_dot(small.astype(BF16), wlr_ref[...]) + blr_ref[...]) * (1.0 / TAU_A)
        if valid < r:
            la = jnp.where(row_ok_all if rows == slice(None) else row_ok, la, 0.0)
        return la

    gated_linear(lambda rows: proj_scr[rows, C_AQ:C_AQ + LA] * DK_A ** -0.5,
                 lambda rows: proj_scr[rows, C_AK:C_AK + LA],
                 lambda rows: proj_scr[rows, C_AV:C_AV + VA],
                 la_a, tria_ref, sgla_scr, pa32_ref, pa16_ref, 0, ca)

    def k_b(rows):
        kb = (1.0 - lb) * _sigmoid(-proj_scr[rows, C_BF:C_BF + LB])
        if valid < r:
            kb = jnp.where(row_ok_all if rows == slice(None) else row_ok, kb, 0.0)
        return kb

    gated_linear(lambda rows: _silu(proj_scr[rows, C_BQ:C_BQ + LB]), k_b,
                 lambda rows: proj_scr[rows, C_BI:C_BI + VB],
                 lambda rows, k=None: jnp.log1p(-(k_b(rows) if k is None else k)),
                 trib_ref, shgrn_scr, pb32_ref, pb16_ref, VA, cb)

    i_pre = proj_scr[:, C_SMALL:C_SMALL + LANES] + ibias_ref[...]
    lsf = _log_sigmoid(proj_scr[:, C_FGATE:C_FGATE + LANES] + fbias_ref[...])
    if valid < r:
        lsf = jnp.where(row_ok_all, lsf, 0.0)
        i_pre = jnp.where(row_ok_all, i_pre, NEG_BIG)
    f_cum = _tri_cumsum(tric_ref[...], lsf)
    a_all = i_pre - f_cum
    causal = (lax.broadcasted_iota(jnp.int32, (r, r), 0) >= lax.broadcasted_iota(jnp.int32, (r, r), 1))
    ones16 = jnp.ones((r, DV_C), BF16)

    def lanes(col):
        return jnp.broadcast_to(col, (col.shape[0], LANES))

    def head_cols(c0, b, h):
        return proj_scr[b * r:(b + 1) * r, c0 + h * DK_C:c0 + (h + 1) * DK_C]

    for b0 in range(0, nb, PAIR_GROUP // H):
        pairs = [(b, h) for b in range(b0, b0 + PAIR_GROUP // H) for h in range(H)]
        a_rows = {b: a_all[b * r:(b + 1) * r].T for b in range(b0, b0 + PAIR_GROUP // H)}
        f_rep = {(b, h): lanes(f_cum[b * r:(b + 1) * r, LANE_G + h:LANE_G + h + 1]) for b, h in pairs}
        a_rep = {(b, h): lanes(a_all[b * r:(b + 1) * r, LANE_G + h:LANE_G + h + 1]) for b, h in pairs}
        m_prev = {(b, h): mrep_scr[b, h:h + 1, :] for b, h in pairs}
        a_mat = {(b, h): jnp.where(causal, a_rows[b][LANE_G + h:LANE_G + h + 1, :], NEG_BIG) for b, h in pairs}
        g = {p: jnp.maximum(lanes(jnp.max(a_mat[p], axis=1, keepdims=True)), m_prev[p]) for p in pairs}
        d = {p: jnp.exp(a_mat[p] - g[p][:, 0:r]) for p in pairs}
        w_prev = {p: jnp.exp(m_prev[p] - g[p]) for p in pairs}
        m_t = {p: f_rep[p] + g[p] for p in pairs}
        q16 = {(b, h): head_cols(C_CQ, b, h).astype(BF16) for b, h in pairs}
        k = {(b, h): head_cols(C_CK, b, h) * DK_C ** -0.5 for b, h in pairs}
        v16 = {(b, h): jnp.concatenate([head_cols(C_CV, b, h).astype(BF16), ones16], axis=1) for b, h in pairs}
        qk = {p: (_dot_nt(q16[p], k[p].astype(BF16)) * d[p]).astype(BF16) for p in pairs}
        c_prev = {(b, h): caug_scr[b, h] for b, h in pairs}
        now = {p: _dot(qk[p], v16[p]) for p in pairs}
        old = {p: _dot(q16[p], c_prev[p].astype(BF16)) for p in pairs}
        for b, h in pairs:
            p = (b, h)
            both = now[p] + jnp.concatenate([w_prev[p], w_prev[p]], axis=1) * old[p]
            mixed_scr[b * r:(b + 1) * r, VA + VB + h * DV_C:VA + VB + (h + 1) * DV_C] = (
                both[:, 0:DV_C] / jnp.maximum(jnp.abs(both[:, DV_C:]), jnp.exp(-m_t[p])))
        g_last = {p: g[p][r - 1:r, :] for p in pairs}
        w_old = {p: jnp.exp(m_prev[p] - g_last[p]) for p in pairs}
        kw = {p: (k[p] * jnp.exp(a_rep[p] - g_last[p])).astype(BF16) for p in pairs}
        upd = {p: _dot_tn(kw[p], v16[p]) for p in pairs}
        for b, h in pairs:
            p = (b, h)
            caug_scr[b, h] = jnp.concatenate([w_old[p], w_old[p]], axis=1) * c_prev[p] + upd[p]
            mrep_scr[b, h:h + 1, :] = m_t[p][r - 1:r, :]

    def fold_states():
        for b in range(nb):
            gla_ref[b] = _fold_heads(sgla_scr[b], DV_A)
            hgrn_ref[b] = _fold_heads(shgrn_scr[b], DV_B)
            for h in range(H):
                c_ref[b, h] = caug_scr[b, h, :, 0:DV_C]
                n_ref[b, h:h + 1, :] = caug_scr[b, h, :, DV_C:].T[0:1, :]
                m_ref[b:b + 1, h:h + 1] = mrep_scr[b, h:h + 1, 0:1]

    if has_init:
        fold_states()
    else:
        pl.when(step == n_steps - 1)(fold_states)

    o_a = mixed_scr[:, 0:VA]
    ms = _dot((o_a * o_a).astype(BF16), pb16_ref[...]) * (1.0 / DV_A)
    mixed_scr[:, 0:VA] = o_a * lax.rsqrt(ms + EPS) * gnorm_ref[...] * _silu(proj_scr[:, C_AG:C_AG + VA])
    o_b = mixed_scr[:, VA:VA + VB]
    ms = _dot((o_b * o_b).astype(BF16), pb16_ref[...]) * (1.0 / DV_B)
    mixed_scr[:, VA:VA + VB] = o_b * lax.rsqrt(ms + EPS) * hnorm_ref[...] * _silu(proj_scr[:, C_BG:C_BG + VB])
    for h in range(H):
        cols = slice(VA + VB + h * DV_C, VA + VB + (h + 1) * DV_C)
        gate = _sigmoid(proj_scr[:, C_CO + h * DV_C:C_CO + (h + 1) * DV_C])
        mixed_scr[:, cols] = gate * _rms(mixed_scr[:, cols], mnorm_ref[...])
    y = _dot(mixed_scr[...].astype(BF16), wout_ref[...])
    y_ref[...] = (x_ref[...].reshape(rows_all, D_MODEL) + _rms(y, gpost_ref[...])).reshape(nb, r, D_MODEL)


def _block_tri(n, block):
    i, j = np.arange(n)[:, None], np.arange(n)[None, :]
    return jnp.asarray((j <= i) & (i // block == j // block), BF16)


def _mixer(layer, x, w, consts, init=None, *, nb, r, valid):
    batch, tokens, _ = x.shape
    has_init = init is not None
    if has_init:
        assert tokens == r
        n_steps = batch // nb
        x_map = lambda i: (i, 0, 0)
        st_map = lambda nd: (lambda i: (i,) + (0,) * (nd - 1))
    else:
        assert batch == nb
        n_steps = tokens // r
        x_map = lambda i: (0, i, 0)
        st_map = lambda nd: (lambda i: (0,) * nd)
    rows_all = nb * r
    slab = min(MXU_TILE, rows_all)
    tris = [_block_tri(r, SUB), _block_tri(slab, min(CHUNK_A, r)), _block_tri(slab, min(CHUNK_B, r)),
            _block_tri(slab, r)]

    state_shapes = [(batch, LA, DV_A), (batch, LB, DV_B), (batch, H, DK_C, DV_C), (batch, H, DK_C), (batch, H)]
    state_specs = [pl.BlockSpec((nb,) + s[1:], st_map(len(s))) for s in state_shapes]
    per_layer = [w["gains"], w["w_in"], w["w_lr"], w["b_lr"], w["gla_norm"]]
    per_layer2 = [w["hgrn_norm"], w["i_bias"], w["f_bias"], w["mlstm_norm"], w["w_out"]]
    shared = [consts["pa32"], consts["pa16"], consts["pb32"], consts["pb16"]] + tris
    in_arrays = [x] + per_layer + [w["lb_logits"]] + per_layer2 + shared
    in_specs = ([pl.BlockSpec((nb, r, D_MODEL), x_map)] + [_layer_spec(layer, a.shape[1:]) for a in per_layer]
                + [_const_spec(w["lb_logits"].shape)] + [_layer_spec(layer, a.shape[1:]) for a in per_layer2]
                + [_const_spec(a.shape) for a in shared])
    if has_init:
        in_arrays += list(init)
        in_specs += [pl.BlockSpec((None, nb) + s[1:], (lambda nd: (lambda i: (layer, i) + (0,) * (nd - 1)))(len(s)))
                     for s in state_shapes]
    scratch = [
        pltpu.VMEM((rows_all, D_MODEL), BF16),
        pltpu.VMEM((rows_all, D_IN_PAD), F32),
        pltpu.VMEM((rows_all, D_MODEL), F32),
        pltpu.VMEM((nb, LA, VA), F32),
        pltpu.VMEM((nb, LB, VB), F32),
        pltpu.VMEM((nb, H, DK_C, 2 * DV_C), F32),
        pltpu.VMEM((nb, H, LANES), F32),
    ]
    state_block = nb * 4 * (LA * LANES + LB * LANES + H * DK_C * DV_C + SUBLANES * LANES + LANES)
    vmem = (2 * (D_MODEL * D_IN_PAD + D_MODEL * D_MODEL)
            + rows_all * (D_MODEL * (2 + 4 + 4 * 4) + D_IN_PAD * 4)
            + nb * 4 * (LA * VA + LB * VB + H * DK_C * 2 * DV_C + SUBLANES * LANES)
            + state_block * (4 if has_init else 2))
    outs = pl.pallas_call(
        functools.partial(_mixer_kernel, layer, nb, r, valid, has_init, n_steps),
        name="token_mixer_sample" if has_init else "token_mixer_prompt",
        out_shape=[jax.ShapeDtypeStruct(x.shape, F32)] + [jax.ShapeDtypeStruct(s, F32) for s in state_shapes],
        grid=(n_steps,),
        in_specs=in_specs,
        out_specs=[pl.BlockSpec((nb, r, D_MODEL), x_map)] + state_specs,
        scratch_shapes=scratch,
        compiler_params=pltpu.CompilerParams(
            dimension_semantics=("arbitrary",),
            vmem_limit_bytes=min(VMEM_BYTES, vmem + 16 * 2**20),
        ),
    )(*in_arrays)
    return outs[0], outs[1:]


def _head_pattern(rows_per_head, cols_per_head):
    rr = np.arange(H * rows_per_head)[:, None] // rows_per_head
    cc = np.arange(H * cols_per_head)[None, :] // cols_per_head
    return (rr == cc).astype(np.float32)


def _constants():
    pa, pb = _head_pattern(DK_A, DV_A), _head_pattern(E_B, DV_B)
    return {"pa32": jnp.asarray(pa), "pa16": jnp.asarray(pa, BF16),
            "pb32": jnp.asarray(pb), "pb16": jnp.asarray(pb, BF16)}


def _mixer_weights(norm_gains, w_in, gla_w_lr, gla_b_lr, gla_norm, hgrn_lb_logits, hgrn_norm,
                   mlstm_i_bias, mlstm_f_bias, mlstm_norm, w_out):
    (a_q, a_k, a_v, a_g, a_lr, b_q, b_f, b_i, b_g, c_q, c_k, c_v, c_o, c_i, c_f) = jnp.split(
        w_in, np.cumsum(SPLIT_SIZES)[:-1].tolist(), axis=2)
    zeros = lambda n: jnp.zeros((DEPTH, D_MODEL, n), F32)
    packed = jnp.concatenate(
        [a_q, a_k, a_v, a_g, b_q, b_f, b_i, b_g, c_q, c_k, c_v, c_o,
         a_lr, c_i, zeros(LANES - RANK_A - H), zeros(LANE_G), c_f, zeros(D_IN_PAD - C_FGATE - LANE_G - H)], axis=2)
    rows = lambda v: v.reshape(DEPTH, 1, -1).astype(F32)
    gate_lanes = lambda v: rows(jnp.pad(v, ((0, 0), (LANE_G, LANES - LANE_G - H))))
    return {
        "gains": norm_gains.astype(F32),
        "w_in": packed.astype(BF16),
        "w_lr": jnp.pad(gla_w_lr, ((0, 0), (0, LANES - RANK_A), (0, 0))).astype(BF16),
        "b_lr": rows(gla_b_lr),
        "gla_norm": rows(jnp.tile(gla_norm, (1, H))),
        "lb_logits": hgrn_lb_logits.astype(F32),
        "hgrn_norm": rows(jnp.tile(hgrn_norm, (1, H))),
        "i_bias": gate_lanes(mlstm_i_bias), "f_bias": gate_lanes(mlstm_f_bias),
        "mlstm_norm": rows(mlstm_norm),
        "w_out": w_out.astype(BF16),
    }


PROMPT_CHUNK = 64
SAMPLE_ROWS = SUBLANES
SAMPLE_GROUP = 8


def kernel(x_prompt, x_sample, state_gla, state_hgrn, state_mlstm_C, state_mlstm_n, state_mlstm_m, norm_gains, ffn1_w_gate, ffn1_w_up, ffn1_w_down, w_in, gla_w_lr, gla_b_lr, gla_norm, hgrn_lb_logits, hgrn_norm, mlstm_i_bias, mlstm_f_bias, mlstm_norm, w_out, ffn2_w_gate, ffn2_w_up, ffn2_w_down):
    bp, tp, _ = x_prompt.shape
    bs, ts, _ = x_sample.shape
    consts = _constants()
    yp = x_prompt.reshape(bp * tp, D_MODEL)
    ys = x_sample.reshape(bs * ts, D_MODEL)
    prompt_states, sample_states = [], []
    mix = _mixer_weights(norm_gains, w_in, gla_w_lr, gla_b_lr, gla_norm, hgrn_lb_logits, hgrn_norm,
                         mlstm_i_bias, mlstm_f_bias, mlstm_norm, w_out)
    gains = mix["gains"]
    ffn1 = tuple(w.astype(BF16) for w in (ffn1_w_gate, ffn1_w_up, ffn1_w_down))
    ffn2 = tuple(w.astype(BF16) for w in (ffn2_w_gate, ffn2_w_up, ffn2_w_down))
    init = (state_gla.reshape(DEPTH, bs, LA, DV_A), state_hgrn.reshape(DEPTH, bs, LB, DV_B),
            state_mlstm_C, state_mlstm_n, state_mlstm_m)
    for l in range(DEPTH):
        yp = _ffn(l, 0, yp, gains, *ffn1)
        ys = _ffn(l, 0, ys, gains, *ffn1)
        yp3, st_p = _mixer(l, yp.reshape(bp, tp, D_MODEL), mix, consts, nb=bp, r=PROMPT_CHUNK, valid=PROMPT_CHUNK)
        ys_pad = jnp.pad(ys.reshape(bs, ts, D_MODEL), ((0, 0), (0, SAMPLE_ROWS - ts), (0, 0)))
        ys3, st_s = _mixer(l, ys_pad, mix, consts, init, nb=SAMPLE_GROUP, r=SAMPLE_ROWS, valid=ts)
        yp = _ffn(l, 4, yp3.reshape(bp * tp, D_MODEL), gains, *ffn2)
        ys = _ffn(l, 4, ys3[:, :ts].reshape(bs * ts, D_MODEL), gains, *ffn2)
        prompt_states.append(st_p)
        sample_states.append(st_s)

    def stack(states, batch):
        gla, hgrn, c, n, m = (jnp.stack(s) for s in zip(*states))
        return (gla.reshape(DEPTH, batch, H, DK_A, DV_A), hgrn.reshape(DEPTH, batch, H, E_B, DV_B), c, n, m)

    return ((yp.reshape(bp, tp, D_MODEL), ys.reshape(bs, ts, D_MODEL))
            + stack(prompt_states, bp) + stack(sample_states, bs))
```

```python
import functools

import jax
import jax.numpy as jnp
import numpy as np
from jax import lax
from jax.experimental import pallas as pl
from jax.experimental.pallas import tpu as pltpu

F32 = jnp.float32
BF16 = jnp.bfloat16

D_MODEL = 1024
D_FF = 2688
DEPTH = 2
H = 4
DK_A, DV_A, RANK_A, TAU_A = 32, 64, 16, 16.0
E_B, DV_B = 64, 64
DK_C, DV_C = 128, 128
EPS = 1e-6
NEG_BIG = -1e30
M_INIT = -1e30
SPLIT_SIZES = (
    H * DK_A, H * DK_A, H * DV_A, H * DV_A, RANK_A,
    H * E_B, H * E_B, H * DV_B, H * DV_B,
    H * DK_C, H * DK_C, H * DV_C, H * DV_C, H, H,
)

LANES = 128
SUBLANES = 8
MXU_TILE = 256
VMEM_BYTES = 64 * 2**20

SUB = SUBLANES
LA, VA = H * DK_A, H * DV_A
LB, VB = H * E_B, H * DV_B
N_FF_TILES, FF_TAIL = divmod(D_FF, MXU_TILE)
FF_TILES_PER_ITER = 2
TOKEN_TILE = 512

C_AQ, C_AK, C_AV, C_AG = 0, LA, 2 * LA, 2 * LA + VA
C_BQ = C_AG + VA
C_BF, C_BI, C_BG = C_BQ + LB, C_BQ + 2 * LB, C_BQ + 3 * LB
C_CQ = C_BG + VB
C_CK, C_CV, C_CO = C_CQ + H * DK_C, C_CQ + 2 * H * DK_C, C_CQ + 2 * H * DK_C + H * DV_C
C_SMALL = C_CO + H * DV_C
C_FGATE = C_SMALL + LANES
N_IN_TILES = pl.cdiv(C_FGATE + LANES, MXU_TILE)
D_IN_PAD = N_IN_TILES * MXU_TILE
IN_TILES_PER_ITER = 4
LANE_G = RANK_A

CHUNK_A, CHUNK_B = 64, 32
PAIR_GROUP = 8
FAST_DECAY_LIMIT = 60.0


def _rms(x, g):
    return x * lax.rsqrt(jnp.mean(x * x, axis=-1, keepdims=True) + EPS) * g


def _sigmoid(x):
    return 1.0 / (1.0 + jnp.exp(-x))


def _silu(x):
    return x * _sigmoid(x)


def _log_sigmoid(x):
    return jnp.minimum(x, 0.0) - jnp.log1p(jnp.exp(-jnp.abs(x)))


def _dot(a, b):
    return jnp.dot(a, b, preferred_element_type=F32)


def _dot_tn(a, b):
    return lax.dot_general(a, b, (((0,), (0,)), ((), ())), preferred_element_type=F32)


def _dot_nt(a, b):
    return lax.dot_general(a, b, (((1,), (1,)), ((), ())), preferred_element_type=F32)


def _tri_cumsum(tri16, x):
    t = tri16.shape[0]
    hi = x.astype(BF16)
    r1 = x - hi.astype(F32)
    mid = r1.astype(BF16)
    lo = (r1 - mid.astype(F32)).astype(BF16)
    out = []
    for s in range(0, x.shape[0], t):
        out.append(_dot(tri16, hi[s:s + t]) + _dot(tri16, mid[s:s + t]) + _dot(tri16, lo[s:s + t]))
    return out[0] if len(out) == 1 else jnp.concatenate(out, axis=0)


def _pad_rows(x, mult):
    pad = -x.shape[0] % mult
    return x if pad == 0 else jnp.concatenate([x, jnp.zeros((pad, x.shape[1]), x.dtype)], axis=0)


def _ffn_kernel(gain_row, x_ref, gains_ref, wg_ref, wu_ref, wd_ref, o_ref, h_scr, acc_scr):
    gpre_ref, gpost_ref = gains_ref.at[gain_row:gain_row + 1, :], gains_ref.at[gain_row + 1:gain_row + 2, :]
    h_scr[...] = _rms(x_ref[...], gpre_ref[...]).astype(BF16)
    acc_scr[...] = jnp.zeros_like(acc_scr)

    def ff_cols(cols):
        h = h_scr[...]
        g = _dot(h, wg_ref[:, cols])
        u = _dot(h, wu_ref[:, cols])
        acc_scr[...] += _dot((_silu(g) * u).astype(BF16), wd_ref[cols, :])

    def ff_tiles(j, carry):
        for u in range(FF_TILES_PER_ITER):
            ff_cols(pl.ds(pl.multiple_of((j * FF_TILES_PER_ITER + u) * MXU_TILE, MXU_TILE), MXU_TILE))
        return carry

    lax.fori_loop(0, N_FF_TILES // FF_TILES_PER_ITER, ff_tiles, 0)
    if FF_TAIL:
        ff_cols(pl.ds(N_FF_TILES * MXU_TILE, FF_TAIL))
    o_ref[...] = x_ref[...] + 0.5 * _rms(acc_scr[...], gpost_ref[...])


def _const_spec(shape):
    return pl.BlockSpec(shape, lambda *_: (0,) * len(shape), pipeline_mode=pl.Buffered(1))


def _layer_spec(layer, shape):
    return pl.BlockSpec((None,) + tuple(shape), lambda *_: (layer,) + (0,) * len(shape),
                        pipeline_mode=pl.Buffered(1))


def _ffn(layer, gain_row, x, gains, wg, wu, wd):
    n = x.shape[0]
    tm = min(TOKEN_TILE, n)
    weights = 3 * D_MODEL * D_FF * 2
    tiles = tm * D_MODEL * (2 * 2 * 4 + 2 + 4) + 4 * tm * MXU_TILE * 4
    return pl.pallas_call(
        functools.partial(_ffn_kernel, gain_row),
        name="swiglu_half_step",
        out_shape=jax.ShapeDtypeStruct((n, D_MODEL), F32),
        grid=(n // tm,),
        in_specs=[
            pl.BlockSpec((tm, D_MODEL), lambda i: (i, 0)),
            _layer_spec(layer, gains.shape[1:]),
            _layer_spec(layer, (D_MODEL, D_FF)),
            _layer_spec(layer, (D_MODEL, D_FF)),
            _layer_spec(layer, (D_FF, D_MODEL)),
        ],
        out_specs=pl.BlockSpec((tm, D_MODEL), lambda i: (i, 0)),
        scratch_shapes=[pltpu.VMEM((tm, D_MODEL), BF16), pltpu.VMEM((tm, D_MODEL), F32)],
        compiler_params=pltpu.CompilerParams(
            dimension_semantics=("arbitrary",),
            vmem_limit_bytes=min(VMEM_BYTES, 2 * (weights + tiles)),
        ),
    )(x, gains, wg, wu, wd)


def _gated_linear_exact(q, k, v, la, s_prev, pat32, pat16, tri_sub16):
    r, l = q.shape
    vv = v.shape[1]
    n_sub = r // SUB
    b = _tri_cumsum(tri_sub16, la)
    b3, q3, k3, v3 = (a.reshape(n_sub, SUB, a.shape[1]) for a in (b, q, k, v))
    bend3 = b3[:, SUB - 1:SUB, :]
    qh3 = q3 * jnp.exp(b3)
    kh3 = k3 * jnp.exp(bend3 - b3)

    tpos = lax.broadcasted_iota(jnp.int32, (n_sub, SUB, l), 1)
    intra3 = jnp.zeros((n_sub, SUB, vv), F32)
    for j in range(SUB):
        arg = jnp.where(tpos >= j, b3 - b3[:, j:j + 1, :], NEG_BIG)
        p = (q3 * k3[:, j:j + 1, :] * jnp.exp(arg)).reshape(r, l)
        sc = _dot(p.astype(BF16), pat16)
        intra3 = intra3 + sc.reshape(n_sub, SUB, vv) * v3[:, j:j + 1, :]

    dcols = jnp.exp(_pad_rows(bend3.reshape(n_sub, l), SUBLANES)).T
    s = s_prev
    inter = []
    for i in range(n_sub):
        inter.append(_dot(qh3[i].astype(BF16), s.astype(BF16)))
        u = _dot_tn(kh3[i].astype(BF16), v3[i].astype(BF16))
        s = s * dcols[:, i:i + 1] + u * pat32
    o = inter[0] if n_sub == 1 else jnp.concatenate(inter, axis=0)
    return o + intra3.reshape(r, vv), s


def _gated_linear_fast(q, k, v, bc, s_scr, o_store, pat32, nb, r, c):
    l, vv = q.shape[1], v.shape[1]
    n_ch = r // c
    n_tot = nb * n_ch
    bc3, q3, k3, v3 = (a.reshape(n_tot, c, a.shape[1]) for a in (bc, q, k, v))
    bmid = bc3[:, c // 2 - 1:c // 2, :]
    blast = bc3[:, c - 1:c, :]
    qt = (q3 * jnp.exp(bc3 - bmid)).astype(BF16)
    kt = (k3 * jnp.exp(bmid - bc3)).astype(BF16)
    qc = (q3 * jnp.exp(bc3)).astype(BF16)
    ke = (k3 * jnp.exp(blast - bc3)).astype(BF16)
    v16 = v3.astype(BF16)
    dcols = jnp.exp(_pad_rows(blast.reshape(n_tot, l), SUBLANES)).T

    head_l = lax.broadcasted_iota(jnp.int32, (c, l), 1) // (l // H)
    head_v = lax.broadcasted_iota(jnp.int32, (c, vv), 1) // (vv // H)
    causal = (lax.broadcasted_iota(jnp.int32, (H * c, c), 0) % c >= lax.broadcasted_iota(jnp.int32, (H * c, c), 1))
    zero16 = jnp.zeros((), BF16)

    idxs = range(n_tot)
    lhs = [jnp.concatenate([jnp.where(head_l == h, qt[i], zero16) for h in range(H)], axis=0) for i in idxs]
    sc = [_dot_nt(lhs[i], kt[i]) for i in idxs]
    sc = [jnp.where(causal, sc[i], 0.0).astype(BF16) for i in idxs]
    ov = [_dot(sc[i], v16[i]) for i in idxs]
    intra = [functools.reduce(jnp.add, [jnp.where(head_v == h, ov[i][h * c:(h + 1) * c], 0.0) for h in range(H)])
             for i in idxs]
    u = [_dot_tn(ke[i], v16[i]) for i in idxs]

    for ci in range(n_ch):
        ids = [b * n_ch + ci for b in range(nb)]
        s_old = [s_scr[b] for b in range(nb)]
        inter = [_dot(qc[i], s_old[b].astype(BF16)) for b, i in enumerate(ids)]
        for b, i in enumerate(ids):
            o_store(b, ci * c, c, intra[i] + inter[b])
            s_scr[b] = s_old[b] * dcols[:, i:i + 1] + u[i] * pat32


def _expand_heads(s_c):
    l, dv = s_c.shape
    dk = l // H
    rows = []
    for h in range(H):
        parts = [jnp.zeros((dk, h * dv), F32), s_c[h * dk:(h + 1) * dk, :], jnp.zeros((dk, (H - 1 - h) * dv), F32)]
        rows.append(jnp.concatenate([p for p in parts if p.shape[1]], axis=1))
    return jnp.concatenate(rows, axis=0)


def _fold_heads(s, dv):
    dk = s.shape[0] // H
    return jnp.concatenate([s[h * dk:(h + 1) * dk, h * dv:(h + 1) * dv] for h in range(H)], axis=0)


def _mixer_kernel(layer, nb, r, valid, has_init, n_steps, *refs):
    (x_ref, gains_ref, win_ref, wlr_ref, blr_ref, gnorm_ref, lb_ref, hnorm_ref,
     ibias_ref, fbias_ref, mnorm_ref, wout_ref, pa32_ref, pa16_ref, pb32_ref, pb16_ref, trisub_ref,
     tria_ref, trib_ref, tric_ref) = refs[:20]
    refs = refs[20:]
    gpre_ref, gpost_ref = gains_ref.at[2:3, :], gains_ref.at[3:4, :]
    if has_init:
        gla0_ref, hgrn0_ref, c0_ref, n0_ref, m0_ref = refs[:5]
        refs = refs[5:]
    if layer:
        earlier_refs = refs[:5]
        refs = refs[5:]
    y_ref = refs[0]
    out_refs = refs[1:6]
    gla_ref, hgrn_ref, c_ref, n_ref, m_ref = (o.at[layer] for o in out_refs)
    hin_scr, proj_scr, mixed_scr, sgla_scr, shgrn_scr, cst_scr, nrow_scr, mrep_scr = refs[6:]
    n_as_column = not has_init
    rows_all = nb * r
    ca, cb = min(CHUNK_A, r), min(CHUNK_B, r)
    step = pl.program_id(0)

    if has_init:
        for b in range(nb):
            sgla_scr[b] = _expand_heads(gla0_ref[b])
            shgrn_scr[b] = _expand_heads(hgrn0_ref[b])
            for h in range(H):
                mrep_scr[b, h:h + 1, :] = jnp.broadcast_to(m0_ref[b:b + 1, h:h + 1], (1, LANES))
        cst_scr[...] = c0_ref[...]
        nrow_scr[...] = n0_ref[...]
    else:
        @pl.when(step == 0)
        def _():
            sgla_scr[...] = jnp.zeros_like(sgla_scr)
            shgrn_scr[...] = jnp.zeros_like(shgrn_scr)
            cst_scr[...] = jnp.zeros_like(cst_scr)
            mrep_scr[...] = jnp.full(mrep_scr.shape, M_INIT, F32)

    def load_x():
        x_in = x_ref[...]
        if valid < r:
            x_in = jnp.concatenate([x_in, jnp.zeros((nb, r - valid, D_MODEL), F32)], axis=1)
        return x_in.reshape(rows_all, D_MODEL)

    hin_scr[...] = _rms(load_x(), gpre_ref[...]).astype(BF16)

    def in_tiles(t, carry):
        for u in range(IN_TILES_PER_ITER):
            cols = pl.ds(pl.multiple_of((t * IN_TILES_PER_ITER + u) * MXU_TILE, MXU_TILE), MXU_TILE)
            proj_scr[:, cols] = _dot(hin_scr[...], win_ref[:, cols])
        return carry

    lax.fori_loop(0, N_IN_TILES // IN_TILES_PER_ITER, in_tiles, 0)

    lg = [lb_ref[j:j + 1, :] for j in range(DEPTH)]
    lg_max = functools.reduce(jnp.maximum, lg)
    ex = [jnp.exp(v - lg_max) for v in lg]
    ex_sum = functools.reduce(jnp.add, ex)
    lb = functools.reduce(jnp.add, [e / ex_sum for e in ex[1:layer + 1]], jnp.zeros_like(lg_max))

    if valid < r:
        row_ok_all = lax.broadcasted_iota(jnp.int32, (rows_all, 1), 0) % r < valid
        row_ok = row_ok_all[0:r]

    def store_cols(c0):
        def store(b, t0, n, val):
            mixed_scr[b * r + t0:b * r + t0 + n, c0:c0 + val.shape[1]] = val
        return store

    def exact_path(q_of, k_of, v_of, la_of, s_scr, pat32_ref, pat16_ref, c0):
        def one(bi, carry):
            rows = pl.ds(pl.multiple_of(bi * r, r), r)
            o, s1 = _gated_linear_exact(q_of(rows), k_of(rows), v_of(rows), la_of(rows), s_scr[bi],
                                        pat32_ref[...], pat16_ref[...], trisub_ref[...])
            mixed_scr[rows, c0:c0 + o.shape[1]] = o
            s_scr[bi] = s1
            return carry
        lax.fori_loop(0, nb, one, 0)

    def gated_linear(q_of, k_of, v_of, la_of, tri_ref, s_scr, pat32_ref, pat16_ref, c0, c):
        everything = slice(None)
        k_all = k_of(everything)
        bc = _tri_cumsum(tri_ref[...], la_of(everything, k_all))
        safe = jnp.max(-bc) <= FAST_DECAY_LIMIT

        @pl.when(safe)
        def _():
            _gated_linear_fast(q_of(everything), k_all, v_of(everything), bc, s_scr,
                               store_cols(c0), pat32_ref[...], nb, r, c)

        @pl.when(jnp.logical_not(safe))
        def _():
            exact_path(q_of, k_of, v_of, la_of, s_scr, pat32_ref, pat16_ref, c0)

    def la_a(rows, k=None):
        small = proj_scr[rows, C_SMALL:C_SMALL + LANES]
        la = _log_sigmoid(_dot(small.astype(BF16), wlr_ref[...]) + blr_ref[...]) * (1.0 / TAU_A)
        if valid < r:
            la = jnp.where(row_ok_all if rows == slice(None) else row_ok, la, 0.0)
        return la

    gated_linear(lambda rows: proj_scr[rows, C_AQ:C_AQ + LA] * DK_A ** -0.5,
                 lambda rows: proj_scr[rows, C_AK:C_AK + LA],
                 lambda rows: proj_scr[rows, C_AV:C_AV + VA],
                 la_a, tria_ref, sgla_scr, pa32_ref, pa16_ref, 0, ca)

    def k_b(rows):
        kb = (1.0 - lb) * _sigmoid(-proj_scr[rows, C_BF:C_BF + LB])
        if valid < r:
            kb = jnp.where(row_ok_all if rows == slice(None) else row_ok, kb, 0.0)
        return kb

    gated_linear(lambda rows: _silu(proj_scr[rows, C_BQ:C_BQ + LB]), k_b,
                 lambda rows: proj_scr[rows, C_BI:C_BI + VB],
                 lambda rows, k=None: jnp.log1p(-(k_b(rows) if k is None else k)),
                 trib_ref, shgrn_scr, pb32_ref, pb16_ref, VA, cb)

    i_pre = proj_scr[:, C_SMALL:C_SMALL + LANES] + ibias_ref[...]
    lsf = _log_sigmoid(proj_scr[:, C_FGATE:C_FGATE + LANES] + fbias_ref[...])
    if valid < r:
        lsf = jnp.where(row_ok_all, lsf, 0.0)
        i_pre = jnp.where(row_ok_all, i_pre, NEG_BIG)
    f_cum = _tri_cumsum(tric_ref[...], lsf)
    a_all = i_pre - f_cum
    causal = (lax.broadcasted_iota(jnp.int32, (r, r), 0) >= lax.broadcasted_iota(jnp.int32, (r, r), 1))
    ones16 = jnp.ones((r, DV_C), BF16)

    def lanes(col):
        return jnp.broadcast_to(col, (col.shape[0], LANES))

    def head_cols(c0, b, h):
        return proj_scr[b * r:(b + 1) * r, c0 + h * DK_C:c0 + (h + 1) * DK_C]

    for b0 in range(0, nb, PAIR_GROUP // H):
        pairs = [(b, h) for b in range(b0, b0 + PAIR_GROUP // H) for h in range(H)]
        a_rows = {b: a_all[b * r:(b + 1) * r].T for b in range(b0, b0 + PAIR_GROUP // H)}
        f_rep = {(b, h): lanes(f_cum[b * r:(b + 1) * r, LANE_G + h:LANE_G + h + 1]) for b, h in pairs}
        a_rep = {(b, h): lanes(a_all[b * r:(b + 1) * r, LANE_G + h:LANE_G + h + 1]) for b, h in pairs}
        m_prev = {(b, h): mrep_scr[b, h:h + 1, :] for b, h in pairs}
        a_mat = {(b, h): jnp.where(causal, a_rows[b][LANE_G + h:LANE_G + h + 1, :], NEG_BIG) for b, h in pairs}
        g = {p: jnp.maximum(lanes(jnp.max(a_mat[p], axis=1, keepdims=True)), m_prev[p]) for p in pairs}
        d = {p: jnp.exp(a_mat[p] - g[p][:, 0:r]) for p in pairs}
        w_prev = {p: jnp.exp(m_prev[p] - g[p]) for p in pairs}
        m_t = {p: f_rep[p] + g[p] for p in pairs}
        q = {(b, h): head_cols(C_CQ, b, h) for b, h in pairs}
        q16 = {p: q[p].astype(BF16) for p in pairs}
        k = {(b, h): head_cols(C_CK, b, h) * DK_C ** -0.5 for b, h in pairs}
        v16 = {(b, h): head_cols(C_CV, b, h).astype(BF16) for b, h in pairs}
        if n_as_column:
            v16 = {p: jnp.concatenate([v16[p], ones16], axis=1) for p in pairs}
        qk = {p: _dot_nt(q16[p], k[p].astype(BF16)) * d[p] for p in pairs}
        c_prev = {(b, h): cst_scr[b, h] for b, h in pairs}
        now = {p: _dot(qk[p].astype(BF16), v16[p]) for p in pairs}
        old = {p: _dot(q16[p], c_prev[p].astype(BF16)) for p in pairs}
        for b, h in pairs:
            p = (b, h)
            if n_as_column:
                both = now[p] + jnp.concatenate([w_prev[p], w_prev[p]], axis=1) * old[p]
                num, den = both[:, 0:DV_C], both[:, DV_C:]
            else:
                num = now[p] + w_prev[p] * old[p]
                den = lanes(jnp.sum(qk[p], axis=1, keepdims=True)) + w_prev[p] * lanes(
                    jnp.sum(q[p] * nrow_scr[b, h:h + 1, :], axis=1, keepdims=True))
            mixed_scr[b * r:(b + 1) * r, VA + VB + h * DV_C:VA + VB + (h + 1) * DV_C] = (
                num / jnp.maximum(jnp.abs(den), jnp.exp(-m_t[p])))
        g_last = {p: g[p][r - 1:r, :] for p in pairs}
        w_old = {p: jnp.exp(m_prev[p] - g_last[p]) for p in pairs}
        kw = {p: k[p] * jnp.exp(a_rep[p] - g_last[p]) for p in pairs}
        upd = {p: _dot_tn(kw[p].astype(BF16), v16[p]) for p in pairs}
        for b, h in pairs:
            p = (b, h)
            if n_as_column:
                cst_scr[b, h] = jnp.concatenate([w_old[p], w_old[p]], axis=1) * c_prev[p] + upd[p]
            else:
                cst_scr[b, h] = w_old[p] * c_prev[p] + upd[p]
                nrow_scr[b, h:h + 1, :] = (w_old[p] * nrow_scr[b, h:h + 1, :]
                                           + jnp.sum(kw[p], axis=0, keepdims=True))
            mrep_scr[b, h:h + 1, :] = m_t[p][r - 1:r, :]

    def fold_states():
        for b in range(nb):
            gla_ref[b] = _fold_heads(sgla_scr[b], DV_A)
            hgrn_ref[b] = _fold_heads(shgrn_scr[b], DV_B)
            for h in range(H):
                c_ref[b, h] = cst_scr[b, h, :, 0:DV_C]
                if n_as_column:
                    n_ref[b, h:h + 1, :] = cst_scr[b, h, :, DV_C:].T[0:1, :]
                m_ref[b:b + 1, h:h + 1] = mrep_scr[b, h:h + 1, 0:1]
            if not n_as_column:
                n_ref[b] = nrow_scr[b]
        if layer:
            for prev, out in zip(earlier_refs, out_refs):
                out[0:layer] = prev[...]

    if has_init:
        fold_states()
    else:
        pl.when(step == n_steps - 1)(fold_states)

    o_a = mixed_scr[:, 0:VA]
    ms = _dot((o_a * o_a).astype(BF16), pb16_ref[...]) * (1.0 / DV_A)
    mixed_scr[:, 0:VA] = o_a * lax.rsqrt(ms + EPS) * gnorm_ref[...] * _silu(proj_scr[:, C_AG:C_AG + VA])
    o_b = mixed_scr[:, VA:VA + VB]
    ms = _dot((o_b * o_b).astype(BF16), pb16_ref[...]) * (1.0 / DV_B)
    mixed_scr[:, VA:VA + VB] = o_b * lax.rsqrt(ms + EPS) * hnorm_ref[...] * _silu(proj_scr[:, C_BG:C_BG + VB])
    for h in range(H):
        cols = slice(VA + VB + h * DV_C, VA + VB + (h + 1) * DV_C)
        gate = _sigmoid(proj_scr[:, C_CO + h * DV_C:C_CO + (h + 1) * DV_C])
        mixed_scr[:, cols] = gate * _rms(mixed_scr[:, cols], mnorm_ref[...])
    y = _dot(mixed_scr[...].astype(BF16), wout_ref[...])
    y_ref[...] = (load_x() + _rms(y, gpost_ref[...])).reshape(nb, r, D_MODEL)[:, 0:valid, :]


def _block_tri(n, block):
    i, j = np.arange(n)[:, None], np.arange(n)[None, :]
    return jnp.asarray((j <= i) & (i // block == j // block), BF16)


def _mixer(layer, x, w, consts, init=None, earlier=None, *, nb, r):
    batch, tokens, _ = x.shape
    has_init = init is not None
    if has_init:
        valid = tokens
        n_steps = batch // nb
        x_map = lambda i: (i, 0, 0)
        b_idx = lambda i: i
    else:
        assert batch == nb
        valid = r
        n_steps = tokens // r
        x_map = lambda i: (0, i, 0)
        b_idx = lambda i: 0
    rows_all = nb * r
    slab = min(MXU_TILE, rows_all)
    tris = [_block_tri(r, SUB), _block_tri(slab, min(CHUNK_A, r)), _block_tri(slab, min(CHUNK_B, r)),
            _block_tri(slab, r)]

    state_shapes = [(batch, LA, DV_A), (batch, LB, DV_B), (batch, H, DK_C, DV_C), (batch, H, DK_C), (batch, H)]

    def stacked_specs(n_layers):
        return [pl.BlockSpec((n_layers, nb) + s[1:],
                             (lambda nd: (lambda i: (0, b_idx(i)) + (0,) * (nd - 1)))(len(s))) for s in state_shapes]
    per_layer = [w["gains"], w["w_in"], w["w_lr"], w["b_lr"], w["gla_norm"]]
    per_layer2 = [w["hgrn_norm"], w["i_bias"], w["f_bias"], w["mlstm_norm"], w["w_out"]]
    shared = [consts["pa32"], consts["pa16"], consts["pb32"], consts["pb16"]] + tris
    in_arrays = [x] + per_layer + [w["lb_logits"]] + per_layer2 + shared
    in_specs = ([pl.BlockSpec((nb, valid, D_MODEL), x_map)] + [_layer_spec(layer, a.shape[1:]) for a in per_layer]
                + [_const_spec(w["lb_logits"].shape)] + [_layer_spec(layer, a.shape[1:]) for a in per_layer2]
                + [_const_spec(a.shape) for a in shared])
    if has_init:
        in_arrays += list(init)
        in_specs += [pl.BlockSpec((None, nb) + s[1:], (lambda nd: (lambda i: (layer, i) + (0,) * (nd - 1)))(len(s)))
                     for s in state_shapes]
    if layer:
        in_arrays += list(earlier)
        in_specs += stacked_specs(layer)
    scratch = [
        pltpu.VMEM((rows_all, D_MODEL), BF16),
        pltpu.VMEM((rows_all, D_IN_PAD), F32),
        pltpu.VMEM((rows_all, D_MODEL), F32),
        pltpu.VMEM((nb, LA, VA), F32),
        pltpu.VMEM((nb, LB, VB), F32),
        pltpu.VMEM((nb, H, DK_C, DV_C if has_init else 2 * DV_C), F32),
        pltpu.VMEM((nb, H, DK_C), F32),
        pltpu.VMEM((nb, H, LANES), F32),
    ]
    state_block = nb * 4 * (LA * LANES + LB * LANES + H * DK_C * DV_C + SUBLANES * LANES + LANES)
    vmem = (2 * (D_MODEL * D_IN_PAD + D_MODEL * D_MODEL)
            + rows_all * (D_MODEL * (2 + 4 + 4 * 4) + D_IN_PAD * 4)
            + nb * 4 * (LA * VA + LB * VB + H * DK_C * 2 * DV_C + 2 * SUBLANES * LANES)
            + state_block * ((2 if has_init else 0) + 2 * layer + 2 * (layer + 1)))
    outs = pl.pallas_call(
        functools.partial(_mixer_kernel, layer, nb, r, valid, has_init, n_steps),
        name="token_mixer_sample" if has_init else "token_mixer_prompt",
        out_shape=([jax.ShapeDtypeStruct(x.shape, F32)]
                   + [jax.ShapeDtypeStruct((layer + 1,) + s, F32) for s in state_shapes]),
        grid=(n_steps,),
        in_specs=in_specs,
        out_specs=[pl.BlockSpec((nb, valid, D_MODEL), x_map)] + stacked_specs(layer + 1),
        scratch_shapes=scratch,
        compiler_params=pltpu.CompilerParams(
            dimension_semantics=("arbitrary",),
            vmem_limit_bytes=min(VMEM_BYTES, vmem + 16 * 2**20),
        ),
    )(*in_arrays)
    return outs[0], outs[1:]


def _head_pattern(rows_per_head, cols_per_head):
    rr = np.arange(H * rows_per_head)[:, None] // rows_per_head
    cc = np.arange(H * cols_per_head)[None, :] // cols_per_head
    return (rr == cc).astype(np.float32)


def _constants():
    pa, pb = _head_pattern(DK_A, DV_A), _head_pattern(E_B, DV_B)
    return {"pa32": jnp.asarray(pa), "pa16": jnp.asarray(pa, BF16),
            "pb32": jnp.asarray(pb), "pb16": jnp.asarray(pb, BF16)}


def _mixer_weights(norm_gains, w_in, gla_w_lr, gla_b_lr, gla_norm, hgrn_lb_logits, hgrn_norm,
                   mlstm_i_bias, mlstm_f_bias, mlstm_norm, w_out):
    (a_q, a_k, a_v, a_g, a_lr, b_q, b_f, b_i, b_g, c_q, c_k, c_v, c_o, c_i, c_f) = jnp.split(
        w_in, np.cumsum(SPLIT_SIZES)[:-1].tolist(), axis=2)
    zeros = lambda n: jnp.zeros((DEPTH, D_MODEL, n), F32)
    packed = jnp.concatenate(
        [a_q, a_k, a_v, a_g, b_q, b_f, b_i, b_g, c_q, c_k, c_v, c_o,
         a_lr, c_i, zeros(LANES - RANK_A - H), zeros(LANE_G), c_f, zeros(D_IN_PAD - C_FGATE - LANE_G - H)], axis=2)
    rows = lambda v: v.reshape(DEPTH, 1, -1).astype(F32)
    gate_lanes = lambda v: rows(jnp.pad(v, ((0, 0), (LANE_G, LANES - LANE_G - H))))
    return {
        "gains": norm_gains.astype(F32),
        "w_in": packed.astype(BF16),
        "w_lr": jnp.pad(gla_w_lr, ((0, 0), (0, LANES - RANK_A), (0, 0))).astype(BF16),
        "b_lr": rows(gla_b_lr),
        "gla_norm": rows(jnp.tile(gla_norm, (1, H))),
        "lb_logits": hgrn_lb_logits.astype(F32),
        "hgrn_norm": rows(jnp.tile(hgrn_norm, (1, H))),
        "i_bias": gate_lanes(mlstm_i_bias), "f_bias": gate_lanes(mlstm_f_bias),
        "mlstm_norm": rows(mlstm_norm),
        "w_out": w_out.astype(BF16),
    }


PROMPT_CHUNK = 64
SAMPLE_ROWS = SUBLANES
SAMPLE_GROUP = 8


def kernel(x_prompt, x_sample, state_gla, state_hgrn, state_mlstm_C, state_mlstm_n, state_mlstm_m, norm_gains, ffn1_w_gate, ffn1_w_up, ffn1_w_down, w_in, gla_w_lr, gla_b_lr, gla_norm, hgrn_lb_logits, hgrn_norm, mlstm_i_bias, mlstm_f_bias, mlstm_norm, w_out, ffn2_w_gate, ffn2_w_up, ffn2_w_down):
    bp, tp, _ = x_prompt.shape
    bs, ts, _ = x_sample.shape
    consts = _constants()
    yp = x_prompt.reshape(bp * tp, D_MODEL)
    ys = x_sample.reshape(bs * ts, D_MODEL)
    st_p = st_s = None
    mix = _mixer_weights(norm_gains, w_in, gla_w_lr, gla_b_lr, gla_norm, hgrn_lb_logits, hgrn_norm,
                         mlstm_i_bias, mlstm_f_bias, mlstm_norm, w_out)
    gains = mix["gains"]
    ffn1 = tuple(w.astype(BF16) for w in (ffn1_w_gate, ffn1_w_up, ffn1_w_down))
    ffn2 = tuple(w.astype(BF16) for w in (ffn2_w_gate, ffn2_w_up, ffn2_w_down))
    init = (state_gla.reshape(DEPTH, bs, LA, DV_A), state_hgrn.reshape(DEPTH, bs, LB, DV_B),
            state_mlstm_C, state_mlstm_n, state_mlstm_m)
    for l in range(DEPTH):
        yp = _ffn(l, 0, yp, gains, *ffn1)
        ys = _ffn(l, 0, ys, gains, *ffn1)
        yp3, st_p = _mixer(l, yp.reshape(bp, tp, D_MODEL), mix, consts, None, st_p, nb=bp, r=PROMPT_CHUNK)
        ys3, st_s = _mixer(l, ys.reshape(bs, ts, D_MODEL), mix, consts, init, st_s, nb=SAMPLE_GROUP, r=SAMPLE_ROWS)
        yp = _ffn(l, 4, yp3.reshape(bp * tp, D_MODEL), gains, *ffn2)
        ys = _ffn(l, 4, ys3.reshape(bs * ts, D_MODEL), gains, *ffn2)

    def head_shapes(states, batch):
        gla, hgrn, c, n, m = states
        return (gla.reshape(DEPTH, batch, H, DK_A, DV_A), hgrn.reshape(DEPTH, batch, H, E_B, DV_B), c, n, m)

    return ((yp.reshape(bp, tp, D_MODEL), ys.reshape(bs, ts, D_MODEL))
            + head_shapes(st_p, bp) + head_shapes(st_s, bs))
```

```python
import functools

import jax
import jax.numpy as jnp
import numpy as np
from jax import lax
from jax.experimental import pallas as pl
from jax.experimental.pallas import tpu as pltpu

F32 = jnp.float32
BF16 = jnp.bfloat16

D_MODEL = 1024
D_FF = 2688
DEPTH = 2
H = 4
DK_A, DV_A, RANK_A, TAU_A = 32, 64, 16, 16.0
E_B, DV_B = 64, 64
DK_C, DV_C = 128, 128
EPS = 1e-6
NEG_BIG = -1e30
M_INIT = -1e30
SPLIT_SIZES = (
    H * DK_A, H * DK_A, H * DV_A, H * DV_A, RANK_A,
    H * E_B, H * E_B, H * DV_B, H * DV_B,
    H * DK_C, H * DK_C, H * DV_C, H * DV_C, H, H,
)

LANES = 128
SUBLANES = 8
MXU_TILE = 256
VMEM_BYTES = 64 * 2**20

SUB = SUBLANES
LA, VA = H * DK_A, H * DV_A
LB, VB = H * E_B, H * DV_B
N_FF_TILES, FF_TAIL = divmod(D_FF, MXU_TILE)
FF_TILES_PER_ITER = 2
TOKEN_TILE = 1024

C_AQ, C_AK, C_AV, C_AG = 0, LA, 2 * LA, 2 * LA + VA
C_BQ = C_AG + VA
C_BF, C_BI, C_BG = C_BQ + LB, C_BQ + 2 * LB, C_BQ + 3 * LB
C_CQ = C_BG + VB
C_CK, C_CV, C_CO = C_CQ + H * DK_C, C_CQ + 2 * H * DK_C, C_CQ + 2 * H * DK_C + H * DV_C
C_SMALL = C_CO + H * DV_C
C_FGATE = C_SMALL + LANES
N_IN_TILES = pl.cdiv(C_FGATE + LANES, MXU_TILE)
D_IN_PAD = N_IN_TILES * MXU_TILE
IN_TILES_PER_ITER = 4
LANE_G = RANK_A

CHUNK_A, CHUNK_B = 64, 32
PAIR_GROUP = 8
FAST_DECAY_LIMIT = 60.0


def _rms(x, g):
    return x * lax.rsqrt(jnp.mean(x * x, axis=-1, keepdims=True) + EPS) * g


def _sigmoid(x):
    return 1.0 / (1.0 + jnp.exp(-x))


def _silu(x):
    return x * _sigmoid(x)


def _log_sigmoid(x):
    return jnp.minimum(x, 0.0) - jnp.log1p(jnp.exp(-jnp.abs(x)))


def _dot(a, b):
    return jnp.dot(a, b, preferred_element_type=F32)


def _dot_tn(a, b):
    return lax.dot_general(a, b, (((0,), (0,)), ((), ())), preferred_element_type=F32)


def _dot_nt(a, b):
    return lax.dot_general(a, b, (((1,), (1,)), ((), ())), preferred_element_type=F32)


def _tri_cumsum(tri16, x):
    t = tri16.shape[0]
    hi = x.astype(BF16)
    r1 = x - hi.astype(F32)
    mid = r1.astype(BF16)
    lo = (r1 - mid.astype(F32)).astype(BF16)
    out = []
    for s in range(0, x.shape[0], t):
        out.append(_dot(tri16, hi[s:s + t]) + _dot(tri16, mid[s:s + t]) + _dot(tri16, lo[s:s + t]))
    return out[0] if len(out) == 1 else jnp.concatenate(out, axis=0)


def _pad_rows(x, mult):
    pad = -x.shape[0] % mult
    return x if pad == 0 else jnp.concatenate([x, jnp.zeros((pad, x.shape[1]), x.dtype)], axis=0)


def _ffn_kernel(gain_row, x_ref, gains_ref, wg_ref, wu_ref, wd_ref, o_ref, h_scr, acc_scr):
    gpre_ref, gpost_ref = gains_ref.at[gain_row:gain_row + 1, :], gains_ref.at[gain_row + 1:gain_row + 2, :]
    h_scr[...] = _rms(x_ref[...], gpre_ref[...]).astype(BF16)

    def ff_cols(cols):
        h = h_scr[...]
        g = _dot(h, wg_ref[:, cols])
        u = _dot(h, wu_ref[:, cols])
        return _dot((_silu(g) * u).astype(BF16), wd_ref[cols, :])

    def ff_tiles(j, carry):
        for u in range(FF_TILES_PER_ITER):
            acc_scr[...] += ff_cols(pl.ds(pl.multiple_of((j * FF_TILES_PER_ITER + u) * MXU_TILE, MXU_TILE), MXU_TILE))
        return carry

    acc_scr[...] = ff_cols(pl.ds(N_FF_TILES * MXU_TILE, FF_TAIL))
    lax.fori_loop(0, N_FF_TILES // FF_TILES_PER_ITER, ff_tiles, 0)
    o_ref[...] = x_ref[...] + _rms(acc_scr[...], 0.5 * gpost_ref[...])


def _const_spec(shape):
    return pl.BlockSpec(shape, lambda *_: (0,) * len(shape), pipeline_mode=pl.Buffered(1))


def _layer_spec(layer, shape):
    return pl.BlockSpec((None,) + tuple(shape), lambda *_: (layer,) + (0,) * len(shape),
                        pipeline_mode=pl.Buffered(1))


def _ffn(layer, gain_row, x, gains, wg, wu, wd):
    n = x.shape[0]
    tm = min(TOKEN_TILE, n)
    weights = 3 * D_MODEL * D_FF * 2
    tiles = tm * D_MODEL * (2 * 2 * 4 + 2 + 4) + 4 * tm * MXU_TILE * 4
    return pl.pallas_call(
        functools.partial(_ffn_kernel, gain_row),
        name="swiglu_half_step",
        out_shape=jax.ShapeDtypeStruct((n, D_MODEL), F32),
        grid=(n // tm,),
        in_specs=[
            pl.BlockSpec((tm, D_MODEL), lambda i: (i, 0)),
            _layer_spec(layer, gains.shape[1:]),
            _layer_spec(layer, (D_MODEL, D_FF)),
            _layer_spec(layer, (D_MODEL, D_FF)),
            _layer_spec(layer, (D_FF, D_MODEL)),
        ],
        out_specs=pl.BlockSpec((tm, D_MODEL), lambda i: (i, 0)),
        scratch_shapes=[pltpu.VMEM((tm, D_MODEL), BF16), pltpu.VMEM((tm, D_MODEL), F32)],
        compiler_params=pltpu.CompilerParams(
            dimension_semantics=("arbitrary",),
            vmem_limit_bytes=min(VMEM_BYTES, 2 * (weights + tiles)),
        ),
    )(x, gains, wg, wu, wd)


def _gated_linear_exact(q, k, v, la, s_prev, pat32, pat16, tri_sub16):
    r, l = q.shape
    vv = v.shape[1]
    n_sub = r // SUB
    b = _tri_cumsum(tri_sub16, la)
    b3, q3, k3, v3 = (a.reshape(n_sub, SUB, a.shape[1]) for a in (b, q, k, v))
    bend3 = b3[:, SUB - 1:SUB, :]
    qh3 = q3 * jnp.exp(b3)
    kh3 = k3 * jnp.exp(bend3 - b3)

    tpos = lax.broadcasted_iota(jnp.int32, (n_sub, SUB, l), 1)
    intra3 = jnp.zeros((n_sub, SUB, vv), F32)
    for j in range(SUB):
        arg = jnp.where(tpos >= j, b3 - b3[:, j:j + 1, :], NEG_BIG)
        p = (q3 * k3[:, j:j + 1, :] * jnp.exp(arg)).reshape(r, l)
        sc = _dot(p.astype(BF16), pat16)
        intra3 = intra3 + sc.reshape(n_sub, SUB, vv) * v3[:, j:j + 1, :]

    dcols = jnp.exp(_pad_rows(bend3.reshape(n_sub, l), SUBLANES)).T
    s = s_prev
    inter = []
    for i in range(n_sub):
        inter.append(_dot(qh3[i].astype(BF16), s.astype(BF16)))
        u = _dot_tn(kh3[i].astype(BF16), v3[i].astype(BF16))
        s = s * dcols[:, i:i + 1] + u * pat32
    o = inter[0] if n_sub == 1 else jnp.concatenate(inter, axis=0)
    return o + intra3.reshape(r, vv), s


def _gated_linear_fast(q, k, v, bc, s_scr, o_store, pat32, nb, r, c):
    l, vv = q.shape[1], v.shape[1]
    n_ch = r // c
    n_tot = nb * n_ch
    bc3, q3, k3, v3 = (a.reshape(n_tot, c, a.shape[1]) for a in (bc, q, k, v))
    bmid = bc3[:, c // 2 - 1:c // 2, :]
    blast = bc3[:, c - 1:c, :]
    qt = (q3 * jnp.exp(bc3 - bmid)).astype(BF16)
    kt = (k3 * jnp.exp(bmid - bc3)).astype(BF16)
    qc = (q3 * jnp.exp(bc3)).astype(BF16)
    ke = (k3 * jnp.exp(blast - bc3)).astype(BF16)
    v16 = v3.astype(BF16)
    dcols = jnp.exp(_pad_rows(blast.reshape(n_tot, l), SUBLANES)).T

    head_l = lax.broadcasted_iota(jnp.int32, (c, l), 1) // (l // H)
    head_v = lax.broadcasted_iota(jnp.int32, (c, vv), 1) // (vv // H)
    causal = (lax.broadcasted_iota(jnp.int32, (H * c, c), 0) % c >= lax.broadcasted_iota(jnp.int32, (H * c, c), 1))
    zero16 = jnp.zeros((), BF16)

    idxs = range(n_tot)
    lhs = [jnp.concatenate([jnp.where(head_l == h, qt[i], zero16) for h in range(H)], axis=0) for i in idxs]
    sc = [_dot_nt(lhs[i], kt[i]) for i in idxs]
    sc = [jnp.where(causal, sc[i], 0.0).astype(BF16) for i in idxs]
    ov = [_dot(sc[i], v16[i]) for i in idxs]
    intra = [functools.reduce(jnp.add, [jnp.where(head_v == h, ov[i][h * c:(h + 1) * c], 0.0) for h in range(H)])
             for i in idxs]
    u = [_dot_tn(ke[i], v16[i]) for i in idxs]

    for ci in range(n_ch):
        ids = [b * n_ch + ci for b in range(nb)]
        s_old = [s_scr[b] for b in range(nb)]
        inter = [_dot(qc[i], s_old[b].astype(BF16)) for b, i in enumerate(ids)]
        for b, i in enumerate(ids):
            o_store(b, ci * c, c, intra[i] + inter[b])
            s_scr[b] = s_old[b] * dcols[:, i:i + 1] + u[i] * pat32


def _expand_heads(per_head):
    dk, dv = per_head[0].shape
    rows = []
    for h in range(H):
        parts = [jnp.zeros((dk, h * dv), F32), per_head[h], jnp.zeros((dk, (H - 1 - h) * dv), F32)]
        rows.append(jnp.concatenate([p for p in parts if p.shape[1]], axis=1))
    return jnp.concatenate(rows, axis=0)


def _head_block(s, h):
    dk, dv = s.shape[0] // H, s.shape[1] // H
    return s[h * dk:(h + 1) * dk, h * dv:(h + 1) * dv]


def _mixer_kernel(layer, nb, r, valid, has_init, n_steps, *refs):
    (x_ref, gains_ref, win_ref, wlr_ref, blr_ref, gnorm_ref, lb_ref, hnorm_ref,
     ibias_ref, fbias_ref, mnorm_ref, wout_ref, pa32_ref, pa16_ref, pb32_ref, pb16_ref, trisub_ref,
     tria_ref, trib_ref, tric_ref) = refs[:20]
    refs = refs[20:]
    gpre_ref, gpost_ref = gains_ref.at[2:3, :], gains_ref.at[3:4, :]
    if has_init:
        gla0_ref, hgrn0_ref, c0_ref, n0_ref, m0_ref = refs[:5]
        refs = refs[5:]
    if layer:
        earlier_refs = refs[:5]
        refs = refs[5:]
    y_ref = refs[0]
    out_refs = refs[1:6]
    gla_ref, hgrn_ref, c_ref, n_ref, m_ref = (o.at[layer] for o in out_refs)
    hin_scr, proj_scr, mixed_scr, sgla_scr, shgrn_scr, cst_scr, nrow_scr, mrep_scr = refs[6:]
    n_as_column = not has_init
    rows_all = nb * r
    ca, cb = min(CHUNK_A, r), min(CHUNK_B, r)
    step = pl.program_id(0)

    if has_init:
        for b in range(nb):
            sgla_scr[b] = _expand_heads([gla0_ref[b, h] for h in range(H)])
            shgrn_scr[b] = _expand_heads([hgrn0_ref[b, h] for h in range(H)])
            for h in range(H):
                mrep_scr[b, h:h + 1, :] = jnp.broadcast_to(m0_ref[b:b + 1, h:h + 1], (1, LANES))
        cst_scr[...] = c0_ref[...]
        nrow_scr[...] = n0_ref[...]
    else:
        @pl.when(step == 0)
        def _():
            sgla_scr[...] = jnp.zeros_like(sgla_scr)
            shgrn_scr[...] = jnp.zeros_like(shgrn_scr)
            cst_scr[...] = jnp.zeros_like(cst_scr)
            mrep_scr[...] = jnp.full(mrep_scr.shape, M_INIT, F32)

    def load_x():
        x_in = x_ref[...]
        if valid < r:
            x_in = jnp.concatenate([x_in, jnp.zeros((nb, r - valid, D_MODEL), F32)], axis=1)
        return x_in.reshape(rows_all, D_MODEL)

    hin_scr[...] = _rms(load_x(), gpre_ref[...]).astype(BF16)

    def in_tiles(t, carry):
        for u in range(IN_TILES_PER_ITER):
            cols = pl.ds(pl.multiple_of((t * IN_TILES_PER_ITER + u) * MXU_TILE, MXU_TILE), MXU_TILE)
            proj_scr[:, cols] = _dot(hin_scr[...], win_ref[:, cols])
        return carry

    lax.fori_loop(0, N_IN_TILES // IN_TILES_PER_ITER, in_tiles, 0)

    lg = [lb_ref[j:j + 1, :] for j in range(DEPTH)]
    lg_max = functools.reduce(jnp.maximum, lg)
    ex = [jnp.exp(v - lg_max) for v in lg]
    ex_sum = functools.reduce(jnp.add, ex)
    lb = functools.reduce(jnp.add, [e / ex_sum for e in ex[1:layer + 1]], jnp.zeros_like(lg_max))

    if valid < r:
        row_ok_all = lax.broadcasted_iota(jnp.int32, (rows_all, 1), 0) % r < valid
        row_ok = row_ok_all[0:r]

    def store_cols(c0):
        def store(b, t0, n, val):
            mixed_scr[b * r + t0:b * r + t0 + n, c0:c0 + val.shape[1]] = val
        return store

    def exact_path(q_of, k_of, v_of, la_of, s_scr, pat32_ref, pat16_ref, c0):
        def one(bi, carry):
            rows = pl.ds(pl.multiple_of(bi * r, r), r)
            o, s1 = _gated_linear_exact(q_of(rows), k_of(rows), v_of(rows), la_of(rows), s_scr[bi],
                                        pat32_ref[...], pat16_ref[...], trisub_ref[...])
            mixed_scr[rows, c0:c0 + o.shape[1]] = o
            s_scr[bi] = s1
            return carry
        lax.fori_loop(0, nb, one, 0)

    def gated_linear(q_of, k_of, v_of, la_of, tri_ref, s_scr, pat32_ref, pat16_ref, c0, c):
        everything = slice(None)
        k_all = k_of(everything)
        bc = _tri_cumsum(tri_ref[...], la_of(everything, k_all))
        safe = jnp.max(-bc) <= FAST_DECAY_LIMIT

        @pl.when(safe)
        def _():
            _gated_linear_fast(q_of(everything), k_all, v_of(everything), bc, s_scr,
                               store_cols(c0), pat32_ref[...], nb, r, c)

        @pl.when(jnp.logical_not(safe))
        def _():
            exact_path(q_of, k_of, v_of, la_of, s_scr, pat32_ref, pat16_ref, c0)

    def la_a(rows, k=None):
        small = proj_scr[rows, C_SMALL:C_SMALL + LANES]
        la = _log_sigmoid(_dot(small.astype(BF16), wlr_ref[...]) + blr_ref[...]) * (1.0 / TAU_A)
        if valid < r:
            la = jnp.where(row_ok_all if rows == slice(None) else row_ok, la, 0.0)
        return la

    gated_linear(lambda rows: proj_scr[rows, C_AQ:C_AQ + LA] * DK_A ** -0.5,
                 lambda rows: proj_scr[rows, C_AK:C_AK + LA],
                 lambda rows: proj_scr[rows, C_AV:C_AV + VA],
                 la_a, tria_ref, sgla_scr, pa32_ref, pa16_ref, 0, ca)

    def k_b(rows):
        kb = (1.0 - lb) * _sigmoid(-proj_scr[rows, C_BF:C_BF + LB])
        if valid < r:
            kb = jnp.where(row_ok_all if rows == slice(None) else row_ok, kb, 0.0)
        return kb

    gated_linear(lambda rows: _silu(proj_scr[rows, C_BQ:C_BQ + LB]), k_b,
                 lambda rows: proj_scr[rows, C_BI:C_BI + VB],
                 lambda rows, k=None: jnp.log1p(-(k_b(rows) if k is None else k)),
                 trib_ref, shgrn_scr, pb32_ref, pb16_ref, VA, cb)

    i_pre = proj_scr[:, C_SMALL:C_SMALL + LANES] + ibias_ref[...]
    lsf = _log_sigmoid(proj_scr[:, C_FGATE:C_FGATE + LANES] + fbias_ref[...])
    if valid < r:
        lsf = jnp.where(row_ok_all, lsf, 0.0)
        i_pre = jnp.where(row_ok_all, i_pre, NEG_BIG)
    f_cum = _tri_cumsum(tric_ref[...], lsf)
    a_all = i_pre - f_cum
    causal = (lax.broadcasted_iota(jnp.int32, (r, r), 0) >= lax.broadcasted_iota(jnp.int32, (r, r), 1))
    ones16 = jnp.ones((r, DV_C), BF16)

    def lanes(col):
        return jnp.broadcast_to(col, (col.shape[0], LANES))

    def head_cols(c0, b, h):
        return proj_scr[b * r:(b + 1) * r, c0 + h * DK_C:c0 + (h + 1) * DK_C]

    for b0 in range(0, nb, PAIR_GROUP // H):
        pairs = [(b, h) for b in range(b0, b0 + PAIR_GROUP // H) for h in range(H)]
        a_rows = {b: a_all[b * r:(b + 1) * r].T for b in range(b0, b0 + PAIR_GROUP // H)}
        f_rep = {(b, h): lanes(f_cum[b * r:(b + 1) * r, LANE_G + h:LANE_G + h + 1]) for b, h in pairs}
        a_rep = {(b, h): lanes(a_all[b * r:(b + 1) * r, LANE_G + h:LANE_G + h + 1]) for b, h in pairs}
        m_prev = {(b, h): mrep_scr[b, h:h + 1, :] for b, h in pairs}
        a_mat = {(b, h): jnp.where(causal, a_rows[b][LANE_G + h:LANE_G + h + 1, :], NEG_BIG) for b, h in pairs}
        g = {p: jnp.maximum(lanes(jnp.max(a_mat[p], axis=1, keepdims=True)), m_prev[p]) for p in pairs}
        d = {p: jnp.exp(a_mat[p] - g[p][:, 0:r]) for p in pairs}
        w_prev = {p: jnp.exp(m_prev[p] - g[p]) for p in pairs}
        m_t = {p: f_rep[p] + g[p] for p in pairs}
        q = {(b, h): head_cols(C_CQ, b, h) for b, h in pairs}
        q16 = {p: q[p].astype(BF16) for p in pairs}
        k = {(b, h): head_cols(C_CK, b, h) * DK_C ** -0.5 for b, h in pairs}
        v16 = {(b, h): head_cols(C_CV, b, h).astype(BF16) for b, h in pairs}
        if n_as_column:
            v16 = {p: jnp.concatenate([v16[p], ones16], axis=1) for p in pairs}
        qk = {p: _dot_nt(q16[p], k[p].astype(BF16)) * d[p] for p in pairs}
        c_prev = {(b, h): cst_scr[b, h] for b, h in pairs}
        now = {p: _dot(qk[p].astype(BF16), v16[p]) for p in pairs}
        old = {p: _dot(q16[p], c_prev[p].astype(BF16)) for p in pairs}
        for b, h in pairs:
            p = (b, h)
            if n_as_column:
                both = now[p] + jnp.concatenate([w_prev[p], w_prev[p]], axis=1) * old[p]
                num, den = both[:, 0:DV_C], both[:, DV_C:]
            else:
                num = now[p] + w_prev[p] * old[p]
                den = lanes(jnp.sum(qk[p], axis=1, keepdims=True)) + w_prev[p] * lanes(
                    jnp.sum(q[p] * nrow_scr[b, h:h + 1, :], axis=1, keepdims=True))
            mixed_scr[b * r:(b + 1) * r, VA + VB + h * DV_C:VA + VB + (h + 1) * DV_C] = (
                num / jnp.maximum(jnp.abs(den), jnp.exp(-m_t[p])))
        g_last = {p: g[p][r - 1:r, :] for p in pairs}
        w_old = {p: jnp.exp(m_prev[p] - g_last[p]) for p in pairs}
        kw = {p: k[p] * jnp.exp(a_rep[p] - g_last[p]) for p in pairs}
        upd = {p: _dot_tn(kw[p].astype(BF16), v16[p]) for p in pairs}
        for b, h in pairs:
            p = (b, h)
            if n_as_column:
                cst_scr[b, h] = jnp.concatenate([w_old[p], w_old[p]], axis=1) * c_prev[p] + upd[p]
            else:
                cst_scr[b, h] = w_old[p] * c_prev[p] + upd[p]
                nrow_scr[b, h:h + 1, :] = (w_old[p] * nrow_scr[b, h:h + 1, :]
                                           + jnp.sum(kw[p], axis=0, keepdims=True))
            mrep_scr[b, h:h + 1, :] = m_t[p][r - 1:r, :]

    def fold_states():
        for b in range(nb):
            for h in range(H):
                gla_ref[b, h] = _head_block(sgla_scr[b], h)
                hgrn_ref[b, h] = _head_block(shgrn_scr[b], h)
                c_ref[b, h] = cst_scr[b, h, :, 0:DV_C]
                if n_as_column:
                    n_ref[b, h:h + 1, :] = cst_scr[b, h, :, DV_C:].T[0:1, :]
                m_ref[b:b + 1, h:h + 1] = mrep_scr[b, h:h + 1, 0:1]
            if not n_as_column:
                n_ref[b] = nrow_scr[b]
        if layer:
            for prev, out in zip(earlier_refs, out_refs):
                out[0:layer] = prev[...]

    if has_init:
        fold_states()
    else:
        pl.when(step == n_steps - 1)(fold_states)

    o_a = mixed_scr[:, 0:VA]
    ms = _dot((o_a * o_a).astype(BF16), pb16_ref[...]) * (1.0 / DV_A)
    mixed_scr[:, 0:VA] = o_a * lax.rsqrt(ms + EPS) * gnorm_ref[...] * _silu(proj_scr[:, C_AG:C_AG + VA])
    o_b = mixed_scr[:, VA:VA + VB]
    ms = _dot((o_b * o_b).astype(BF16), pb16_ref[...]) * (1.0 / DV_B)
    mixed_scr[:, VA:VA + VB] = o_b * lax.rsqrt(ms + EPS) * hnorm_ref[...] * _silu(proj_scr[:, C_BG:C_BG + VB])
    for h in range(H):
        cols = slice(VA + VB + h * DV_C, VA + VB + (h + 1) * DV_C)
        gate = _sigmoid(proj_scr[:, C_CO + h * DV_C:C_CO + (h + 1) * DV_C])
        mixed_scr[:, cols] = gate * _rms(mixed_scr[:, cols], mnorm_ref[...])
    y = _dot(mixed_scr[...].astype(BF16), wout_ref[...])
    y_ref[...] = (load_x() + _rms(y, gpost_ref[...])).reshape(nb, r, D_MODEL)[:, 0:valid, :]


def _block_tri(n, block):
    i, j = np.arange(n)[:, None], np.arange(n)[None, :]
    return jnp.asarray((j <= i) & (i // block == j // block), BF16)


def _mixer(layer, x, w, consts, init=None, earlier=None, *, nb, r):
    batch, tokens, _ = x.shape
    has_init = init is not None
    if has_init:
        valid = tokens
        n_steps = batch // nb
        x_map = lambda i: (i, 0, 0)
        b_idx = lambda i: i
    else:
        assert batch == nb
        valid = r
        n_steps = tokens // r
        x_map = lambda i: (0, i, 0)
        b_idx = lambda i: 0
    rows_all = nb * r
    slab = min(MXU_TILE, rows_all)
    tris = [_block_tri(r, SUB), _block_tri(slab, min(CHUNK_A, r)), _block_tri(slab, min(CHUNK_B, r)),
            _block_tri(slab, r)]

    state_shapes = [(batch, H, DK_A, DV_A), (batch, H, E_B, DV_B), (batch, H, DK_C, DV_C), (batch, H, DK_C),
                    (batch, H)]

    def stacked_specs(n_layers):
        return [pl.BlockSpec((n_layers, nb) + s[1:],
                             (lambda nd: (lambda i: (0, b_idx(i)) + (0,) * (nd - 1)))(len(s))) for s in state_shapes]
    per_layer = [w["gains"], w["w_in"], w["w_lr"], w["b_lr"], w["gla_norm"]]
    per_layer2 = [w["hgrn_norm"], w["i_bias"], w["f_bias"], w["mlstm_norm"], w["w_out"]]
    shared = [consts["pa32"], consts["pa16"], consts["pb32"], consts["pb16"]] + tris
    in_arrays = [x] + per_layer + [w["lb_logits"]] + per_layer2 + shared
    in_specs = ([pl.BlockSpec((nb, valid, D_MODEL), x_map)] + [_layer_spec(layer, a.shape[1:]) for a in per_layer]
                + [_const_spec(w["lb_logits"].shape)] + [_layer_spec(layer, a.shape[1:]) for a in per_layer2]
                + [_const_spec(a.shape) for a in shared])
    if has_init:
        in_arrays += list(init)
        in_specs += [pl.BlockSpec((None, nb) + s[1:], (lambda nd: (lambda i: (layer, i) + (0,) * (nd - 1)))(len(s)))
                     for s in state_shapes]
    if layer:
        in_arrays += list(earlier)
        in_specs += stacked_specs(layer)
    scratch = [
        pltpu.VMEM((rows_all, D_MODEL), BF16),
        pltpu.VMEM((rows_all, D_IN_PAD), F32),
        pltpu.VMEM((rows_all, D_MODEL), F32),
        pltpu.VMEM((nb, LA, VA), F32),
        pltpu.VMEM((nb, LB, VB), F32),
        pltpu.VMEM((nb, H, DK_C, DV_C if has_init else 2 * DV_C), F32),
        pltpu.VMEM((nb, H, DK_C), F32),
        pltpu.VMEM((nb, H, LANES), F32),
    ]
    state_block = nb * 4 * (LA * LANES + LB * LANES + H * DK_C * DV_C + SUBLANES * LANES + LANES)
    vmem = (2 * (D_MODEL * D_IN_PAD + D_MODEL * D_MODEL)
            + rows_all * (D_MODEL * (2 + 4 + 4 * 4) + D_IN_PAD * 4)
            + nb * 4 * (LA * VA + LB * VB + H * DK_C * 2 * DV_C + 2 * SUBLANES * LANES)
            + state_block * ((2 if has_init else 0) + 2 * layer + 2 * (layer + 1)))
    outs = pl.pallas_call(
        functools.partial(_mixer_kernel, layer, nb, r, valid, has_init, n_steps),
        name="token_mixer_sample" if has_init else "token_mixer_prompt",
        out_shape=([jax.ShapeDtypeStruct(x.shape, F32)]
                   + [jax.ShapeDtypeStruct((layer + 1,) + s, F32) for s in state_shapes]),
        grid=(n_steps,),
        in_specs=in_specs,
        out_specs=[pl.BlockSpec((nb, valid, D_MODEL), x_map)] + stacked_specs(layer + 1),
        scratch_shapes=scratch,
        compiler_params=pltpu.CompilerParams(
            dimension_semantics=("arbitrary",),
            vmem_limit_bytes=min(VMEM_BYTES, vmem + 16 * 2**20),
        ),
    )(*in_arrays)
    return outs[0], outs[1:]


def _head_pattern(rows_per_head, cols_per_head):
    rr = np.arange(H * rows_per_head)[:, None] // rows_per_head
    cc = np.arange(H * cols_per_head)[None, :] // cols_per_head
    return (rr == cc).astype(np.float32)


def _constants():
    pa, pb = _head_pattern(DK_A, DV_A), _head_pattern(E_B, DV_B)
    return {"pa32": jnp.asarray(pa), "pa16": jnp.asarray(pa, BF16),
            "pb32": jnp.asarray(pb), "pb16": jnp.asarray(pb, BF16)}


def _mixer_weights(norm_gains, w_in, gla_w_lr, gla_b_lr, gla_norm, hgrn_lb_logits, hgrn_norm,
                   mlstm_i_bias, mlstm_f_bias, mlstm_norm, w_out):
    (a_q, a_k, a_v, a_g, a_lr, b_q, b_f, b_i, b_g, c_q, c_k, c_v, c_o, c_i, c_f) = jnp.split(
        w_in, np.cumsum(SPLIT_SIZES)[:-1].tolist(), axis=2)
    zeros = lambda n: jnp.zeros((DEPTH, D_MODEL, n), F32)
    packed = jnp.concatenate(
        [a_q, a_k, a_v, a_g, b_q, b_f, b_i, b_g, c_q, c_k, c_v, c_o,
         a_lr, c_i, zeros(LANES - RANK_A - H), zeros(LANE_G), c_f, zeros(D_IN_PAD - C_FGATE - LANE_G - H)], axis=2)
    rows = lambda v: v.reshape(DEPTH, 1, -1).astype(F32)
    gate_lanes = lambda v: rows(jnp.pad(v, ((0, 0), (LANE_G, LANES - LANE_G - H))))
    return {
        "gains": norm_gains.astype(F32),
        "w_in": packed.astype(BF16),
        "w_lr": jnp.pad(gla_w_lr, ((0, 0), (0, LANES - RANK_A), (0, 0))).astype(BF16),
        "b_lr": rows(gla_b_lr),
        "gla_norm": rows(jnp.tile(gla_norm, (1, H))),
        "lb_logits": hgrn_lb_logits.astype(F32),
        "hgrn_norm": rows(jnp.tile(hgrn_norm, (1, H))),
        "i_bias": gate_lanes(mlstm_i_bias), "f_bias": gate_lanes(mlstm_f_bias),
        "mlstm_norm": rows(mlstm_norm),
        "w_out": w_out.astype(BF16),
    }


PROMPT_CHUNK = 64
SAMPLE_ROWS = SUBLANES
SAMPLE_GROUP = 8


def kernel(x_prompt, x_sample, state_gla, state_hgrn, state_mlstm_C, state_mlstm_n, state_mlstm_m, norm_gains, ffn1_w_gate, ffn1_w_up, ffn1_w_down, w_in, gla_w_lr, gla_b_lr, gla_norm, hgrn_lb_logits, hgrn_norm, mlstm_i_bias, mlstm_f_bias, mlstm_norm, w_out, ffn2_w_gate, ffn2_w_up, ffn2_w_down):
    bp, tp, _ = x_prompt.shape
    bs, ts, _ = x_sample.shape
    consts = _constants()
    yp = x_prompt.reshape(bp * tp, D_MODEL)
    ys = x_sample.reshape(bs * ts, D_MODEL)
    st_p = st_s = None
    mix = _mixer_weights(norm_gains, w_in, gla_w_lr, gla_b_lr, gla_norm, hgrn_lb_logits, hgrn_norm,
                         mlstm_i_bias, mlstm_f_bias, mlstm_norm, w_out)
    gains = mix["gains"]
    ffn1 = tuple(w.astype(BF16) for w in (ffn1_w_gate, ffn1_w_up, ffn1_w_down))
    ffn2 = tuple(w.astype(BF16) for w in (ffn2_w_gate, ffn2_w_up, ffn2_w_down))
    init = (state_gla, state_hgrn, state_mlstm_C, state_mlstm_n, state_mlstm_m)
    for l in range(DEPTH):
        yp = _ffn(l, 0, yp, gains, *ffn1)
        ys = _ffn(l, 0, ys, gains, *ffn1)
        yp3, st_p = _mixer(l, yp.reshape(bp, tp, D_MODEL), mix, consts, None, st_p, nb=bp, r=PROMPT_CHUNK)
        ys3, st_s = _mixer(l, ys.reshape(bs, ts, D_MODEL), mix, consts, init, st_s, nb=SAMPLE_GROUP, r=SAMPLE_ROWS)
        yp = _ffn(l, 4, yp3.reshape(bp * tp, D_MODEL), gains, *ffn2)
        ys = _ffn(l, 4, ys3.reshape(bs * ts, D_MODEL), gains, *ffn2)

    return (yp.reshape(bp, tp, D_MODEL), ys.reshape(bs, ts, D_MODEL)) + tuple(st_p) + tuple(st_s)
```

```python
import functools

import jax
import jax.numpy as jnp
import numpy as np
from jax import lax
from jax.experimental import pallas as pl
from jax.experimental.pallas import tpu as pltpu

F32 = jnp.float32
BF16 = jnp.bfloat16

D_MODEL = 1024
D_FF = 2688
DEPTH = 2
H = 4
DK_A, DV_A, RANK_A, TAU_A = 32, 64, 16, 16.0
E_B, DV_B = 64, 64
DK_C, DV_C = 128, 128
EPS = 1e-6
NEG_BIG = -1e30
M_INIT = -1e30
SPLIT_SIZES = (
    H * DK_A, H * DK_A, H * DV_A, H * DV_A, RANK_A,
    H * E_B, H * E_B, H * DV_B, H * DV_B,
    H * DK_C, H * DK_C, H * DV_C, H * DV_C, H, H,
)

LANES = 128
SUBLANES = 8
MXU_TILE = 256
VMEM_BYTES = 64 * 2**20

SUB = SUBLANES
LA, VA = H * DK_A, H * DV_A
LB, VB = H * E_B, H * DV_B
N_FF_TILES, FF_TAIL = divmod(D_FF, MXU_TILE)
FF_TILES_PER_ITER = 2
TOKEN_TILE = 1024

C_AQ, C_AK, C_AV, C_AG = 0, LA, 2 * LA, 2 * LA + VA
C_BQ = C_AG + VA
C_BF, C_BI, C_BG = C_BQ + LB, C_BQ + 2 * LB, C_BQ + 3 * LB
C_CQ = C_BG + VB
C_CK, C_CV, C_CO = C_CQ + H * DK_C, C_CQ + 2 * H * DK_C, C_CQ + 2 * H * DK_C + H * DV_C
C_SMALL = C_CO + H * DV_C
C_FGATE = C_SMALL + LANES
N_IN_TILES = pl.cdiv(C_FGATE + LANES, MXU_TILE)
D_IN_PAD = N_IN_TILES * MXU_TILE
IN_TILES_PER_ITER = 8
LANE_G = RANK_A

CHUNK_A, CHUNK_B = 64, 64
PAIR_GROUP = 8
FAST_DECAY_LIMIT = 60.0


def _rms(x, g):
    return x * lax.rsqrt(jnp.mean(x * x, axis=-1, keepdims=True) + EPS) * g


def _sigmoid(x):
    return 1.0 / (1.0 + jnp.exp(-x))


def _silu(x):
    return x * _sigmoid(x)


def _log_sigmoid(x):
    return jnp.minimum(x, 0.0) - jnp.log1p(jnp.exp(-jnp.abs(x)))


def _dot(a, b):
    return jnp.dot(a, b, preferred_element_type=F32)


def _dot_tn(a, b):
    return lax.dot_general(a, b, (((0,), (0,)), ((), ())), preferred_element_type=F32)


def _dot_nt(a, b):
    return lax.dot_general(a, b, (((1,), (1,)), ((), ())), preferred_element_type=F32)


def _tri_cumsum(tri16, x):
    t = tri16.shape[0]
    hi = x.astype(BF16)
    r1 = x - hi.astype(F32)
    mid = r1.astype(BF16)
    lo = (r1 - mid.astype(F32)).astype(BF16)
    out = []
    for s in range(0, x.shape[0], t):
        out.append(_dot(tri16, hi[s:s + t]) + _dot(tri16, mid[s:s + t]) + _dot(tri16, lo[s:s + t]))
    return out[0] if len(out) == 1 else jnp.concatenate(out, axis=0)


def _pad_rows(x, mult):
    pad = -x.shape[0] % mult
    return x if pad == 0 else jnp.concatenate([x, jnp.zeros((pad, x.shape[1]), x.dtype)], axis=0)


def _ffn_kernel(gain_row, x_ref, gains_ref, wg_ref, wu_ref, wd_ref, o_ref, h_scr, acc_scr):
    gpre_ref, gpost_ref = gains_ref.at[gain_row:gain_row + 1, :], gains_ref.at[gain_row + 1:gain_row + 2, :]
    h_scr[...] = _rms(x_ref[...], gpre_ref[...]).astype(BF16)

    def ff_cols(cols):
        h = h_scr[...]
        g = _dot(h, wg_ref[:, cols])
        u = _dot(h, wu_ref[:, cols])
        return _dot((_silu(g) * u).astype(BF16), wd_ref[cols, :])

    def ff_tiles(j, carry):
        for u in range(FF_TILES_PER_ITER):
            acc_scr[...] += ff_cols(pl.ds(pl.multiple_of((j * FF_TILES_PER_ITER + u) * MXU_TILE, MXU_TILE), MXU_TILE))
        return carry

    acc_scr[...] = ff_cols(pl.ds(N_FF_TILES * MXU_TILE, FF_TAIL))
    lax.fori_loop(0, N_FF_TILES // FF_TILES_PER_ITER, ff_tiles, 0)
    o_ref[...] = x_ref[...] + _rms(acc_scr[...], 0.5 * gpost_ref[...])


def _const_spec(shape):
    return pl.BlockSpec(shape, lambda *_: (0,) * len(shape), pipeline_mode=pl.Buffered(1))


def _layer_spec(layer, shape):
    return pl.BlockSpec((None,) + tuple(shape), lambda *_: (layer,) + (0,) * len(shape),
                        pipeline_mode=pl.Buffered(1))


def _ffn(layer, gain_row, x, gains, wg, wu, wd):
    n = x.shape[0]
    tm = min(TOKEN_TILE, n)
    weights = 3 * D_MODEL * D_FF * 2
    tiles = tm * D_MODEL * (2 * 2 * 4 + 2 + 4) + 4 * tm * MXU_TILE * 4
    return pl.pallas_call(
        functools.partial(_ffn_kernel, gain_row),
        name="swiglu_half_step",
        out_shape=jax.ShapeDtypeStruct((n, D_MODEL), F32),
        grid=(n // tm,),
        in_specs=[
            pl.BlockSpec((tm, D_MODEL), lambda i: (i, 0)),
            _layer_spec(layer, gains.shape[1:]),
            _layer_spec(layer, (D_MODEL, D_FF)),
            _layer_spec(layer, (D_MODEL, D_FF)),
            _layer_spec(layer, (D_FF, D_MODEL)),
        ],
        out_specs=pl.BlockSpec((tm, D_MODEL), lambda i: (i, 0)),
        scratch_shapes=[pltpu.VMEM((tm, D_MODEL), BF16), pltpu.VMEM((tm, D_MODEL), F32)],
        compiler_params=pltpu.CompilerParams(
            dimension_semantics=("arbitrary",),
            vmem_limit_bytes=min(VMEM_BYTES, weights + tiles + 16 * 2**20),
        ),
    )(x, gains, wg, wu, wd)


def _gated_linear_exact(q, k, v, la, s_prev, pat32, pat16, tri_sub16):
    r, l = q.shape
    vv = v.shape[1]
    n_sub = r // SUB
    b = _tri_cumsum(tri_sub16, la)
    b3, q3, k3, v3 = (a.reshape(n_sub, SUB, a.shape[1]) for a in (b, q, k, v))
    bend3 = b3[:, SUB - 1:SUB, :]
    qh3 = q3 * jnp.exp(b3)
    kh3 = k3 * jnp.exp(bend3 - b3)

    tpos = lax.broadcasted_iota(jnp.int32, (n_sub, SUB, l), 1)
    intra3 = jnp.zeros((n_sub, SUB, vv), F32)
    for j in range(SUB):
        arg = jnp.where(tpos >= j, b3 - b3[:, j:j + 1, :], NEG_BIG)
        p = (q3 * k3[:, j:j + 1, :] * jnp.exp(arg)).reshape(r, l)
        sc = _dot(p.astype(BF16), pat16)
        intra3 = intra3 + sc.reshape(n_sub, SUB, vv) * v3[:, j:j + 1, :]

    dcols = jnp.exp(_pad_rows(bend3.reshape(n_sub, l), SUBLANES)).T
    s = s_prev
    inter = []
    for i in range(n_sub):
        inter.append(_dot(qh3[i].astype(BF16), s.astype(BF16)))
        u = _dot_tn(kh3[i].astype(BF16), v3[i].astype(BF16))
        s = s * dcols[:, i:i + 1] + u * pat32
    o = inter[0] if n_sub == 1 else jnp.concatenate(inter, axis=0)
    return o + intra3.reshape(r, vv), s


def _gated_linear_fast(q, k, v, bc, s_scr, o_store, pat32, nb, r, c):
    l, vv = q.shape[1], v.shape[1]
    n_ch = r // c
    n_tot = nb * n_ch
    bc3, q3, k3, v3 = (a.reshape(n_tot, c, a.shape[1]) for a in (bc, q, k, v))
    bmid = bc3[:, c // 2 - 1:c // 2, :]
    blast = bc3[:, c - 1:c, :]
    qt = (q3 * jnp.exp(bc3 - bmid)).astype(BF16)
    kt = (k3 * jnp.exp(bmid - bc3)).astype(BF16)
    qc = (q3 * jnp.exp(bc3)).astype(BF16)
    ke = (k3 * jnp.exp(blast - bc3)).astype(BF16)
    v16 = v3.astype(BF16)
    dcols = jnp.exp(_pad_rows(blast.reshape(n_tot, l), SUBLANES)).T

    head_l = lax.broadcasted_iota(jnp.int32, (c, l), 1) // (l // H)
    head_v = lax.broadcasted_iota(jnp.int32, (c, vv), 1) // (vv // H)
    causal = (lax.broadcasted_iota(jnp.int32, (H * c, c), 0) % c >= lax.broadcasted_iota(jnp.int32, (H * c, c), 1))
    zero16 = jnp.zeros((), BF16)

    idxs = range(n_tot)
    lhs = [jnp.concatenate([jnp.where(head_l == h, qt[i], zero16) for h in range(H)], axis=0) for i in idxs]
    sc = [_dot_nt(lhs[i], kt[i]) for i in idxs]
    sc = [jnp.where(causal, sc[i], 0.0).astype(BF16) for i in idxs]
    ov = [_dot(sc[i], v16[i]) for i in idxs]
    intra = [functools.reduce(jnp.add, [jnp.where(head_v == h, ov[i][h * c:(h + 1) * c], 0.0) for h in range(H)])
             for i in idxs]
    u = [_dot_tn(ke[i], v16[i]) for i in idxs]

    for ci in range(n_ch):
        ids = [b * n_ch + ci for b in range(nb)]
        s_old = [s_scr[b] for b in range(nb)]
        inter = [_dot(qc[i], s_old[b].astype(BF16)) for b, i in enumerate(ids)]
        for b, i in enumerate(ids):
            o_store(b, ci * c, c, intra[i] + inter[b])
            s_scr[b] = s_old[b] * dcols[:, i:i + 1] + u[i] * pat32


def _expand_heads(per_head):
    dk, dv = per_head[0].shape
    rows = []
    for h in range(H):
        parts = [jnp.zeros((dk, h * dv), F32), per_head[h], jnp.zeros((dk, (H - 1 - h) * dv), F32)]
        rows.append(jnp.concatenate([p for p in parts if p.shape[1]], axis=1))
    return jnp.concatenate(rows, axis=0)


def _head_block(s, h):
    dk, dv = s.shape[0] // H, s.shape[1] // H
    return s[h * dk:(h + 1) * dk, h * dv:(h + 1) * dv]


def _mixer_kernel(layer, nb, r, valid, has_init, n_steps, *refs):
    (x_ref, gains_ref, win_ref, wlr_ref, blr_ref, gnorm_ref, lb_ref, hnorm_ref,
     ibias_ref, fbias_ref, mnorm_ref, wout_ref, pa32_ref, pa16_ref, pb32_ref, pb16_ref, trisub_ref,
     tria_ref, trib_ref, tric_ref) = refs[:20]
    refs = refs[20:]
    gpre_ref, gpost_ref = gains_ref.at[2:3, :], gains_ref.at[3:4, :]
    if has_init:
        gla0_ref, hgrn0_ref, c0_ref, n0_ref, m0_ref = refs[:5]
        refs = refs[5:]
    if layer:
        earlier_refs = refs[:5]
        refs = refs[5:]
    y_ref = refs[0]
    out_refs = refs[1:6]
    gla_ref, hgrn_ref, c_ref, n_ref, m_ref = (o.at[layer] for o in out_refs)
    hin_scr, proj_scr, mixed_scr, sgla_scr, shgrn_scr, cst_scr, nrow_scr, mrep_scr = refs[6:]
    n_as_column = not has_init
    rows_all = nb * r
    ca, cb = min(CHUNK_A, r), min(CHUNK_B, r)
    step = pl.program_id(0)

    if has_init:
        for b in range(nb):
            sgla_scr[b] = _expand_heads([gla0_ref[b, h] for h in range(H)])
            shgrn_scr[b] = _expand_heads([hgrn0_ref[b, h] for h in range(H)])
            for h in range(H):
                mrep_scr[b, h:h + 1, :] = jnp.broadcast_to(m0_ref[b:b + 1, h:h + 1], (1, LANES))
        cst_scr[...] = c0_ref[...]
        nrow_scr[...] = n0_ref[...]
    else:
        @pl.when(step == 0)
        def _():
            sgla_scr[...] = jnp.zeros_like(sgla_scr)
            shgrn_scr[...] = jnp.zeros_like(shgrn_scr)
            cst_scr[...] = jnp.zeros_like(cst_scr)
            mrep_scr[...] = jnp.full(mrep_scr.shape, M_INIT, F32)

    def load_x():
        if valid == r:
            return x_ref[...].reshape(rows_all, D_MODEL)
        x_in = x_ref[...]
        pad = jnp.zeros((r - valid, D_MODEL), F32)
        return jnp.concatenate([p for b in range(nb) for p in (x_in[b * valid:(b + 1) * valid], pad)], axis=0)

    hin_scr[...] = _rms(load_x(), gpre_ref[...]).astype(BF16)

    def in_tiles(t, carry):
        for u in range(IN_TILES_PER_ITER):
            cols = pl.ds(pl.multiple_of((t * IN_TILES_PER_ITER + u) * MXU_TILE, MXU_TILE), MXU_TILE)
            proj_scr[:, cols] = _dot(hin_scr[...], win_ref[:, cols])
        return carry

    lax.fori_loop(0, N_IN_TILES // IN_TILES_PER_ITER, in_tiles, 0)

    lg = [lb_ref[j:j + 1, :] for j in range(DEPTH)]
    lg_max = functools.reduce(jnp.maximum, lg)
    ex = [jnp.exp(v - lg_max) for v in lg]
    ex_sum = functools.reduce(jnp.add, ex)
    lb = functools.reduce(jnp.add, [e / ex_sum for e in ex[1:layer + 1]], jnp.zeros_like(lg_max))

    if valid < r:
        row_ok_all = lax.broadcasted_iota(jnp.int32, (rows_all, 1), 0) % r < valid
        row_ok = row_ok_all[0:r]

    def store_cols(c0):
        def store(b, t0, n, val):
            mixed_scr[b * r + t0:b * r + t0 + n, c0:c0 + val.shape[1]] = val
        return store

    def exact_path(q_of, k_of, v_of, la_of, s_scr, pat32_ref, pat16_ref, c0):
        def one(bi, carry):
            rows = pl.ds(pl.multiple_of(bi * r, r), r)
            o, s1 = _gated_linear_exact(q_of(rows), k_of(rows), v_of(rows), la_of(rows), s_scr[bi],
                                        pat32_ref[...], pat16_ref[...], trisub_ref[...])
            mixed_scr[rows, c0:c0 + o.shape[1]] = o
            s_scr[bi] = s1
            return carry
        lax.fori_loop(0, nb, one, 0)

    def gated_linear(q_of, k_of, v_of, la_of, tri_ref, s_scr, pat32_ref, pat16_ref, c0, c):
        everything = slice(None)
        k_all = k_of(everything)
        bc = _tri_cumsum(tri_ref[...], la_of(everything, k_all))
        bc3 = bc.reshape(rows_all // c, c, bc.shape[1])
        to_mid, past_mid = -bc3[:, c // 2 - 1, :], bc3[:, c // 2 - 1, :] - bc3[:, c - 1, :]
        safe = jnp.max(jnp.maximum(to_mid, past_mid)) <= FAST_DECAY_LIMIT

        @pl.when(safe)
        def _():
            _gated_linear_fast(q_of(everything), k_all, v_of(everything), bc, s_scr,
                               store_cols(c0), pat32_ref[...], nb, r, c)

        @pl.when(jnp.logical_not(safe))
        def _():
            exact_path(q_of, k_of, v_of, la_of, s_scr, pat32_ref, pat16_ref, c0)

    def la_a(rows, k=None):
        small = proj_scr[rows, C_SMALL:C_SMALL + LANES]
        la = _log_sigmoid(_dot(small.astype(BF16), wlr_ref[...]) + blr_ref[...]) * (1.0 / TAU_A)
        if valid < r:
            la = jnp.where(row_ok_all if rows == slice(None) else row_ok, la, 0.0)
        return la

    gated_linear(lambda rows: proj_scr[rows, C_AQ:C_AQ + LA] * DK_A ** -0.5,
                 lambda rows: proj_scr[rows, C_AK:C_AK + LA],
                 lambda rows: proj_scr[rows, C_AV:C_AV + VA],
                 la_a, tria_ref, sgla_scr, pa32_ref, pa16_ref, 0, ca)

    def k_b(rows):
        kb = (1.0 - lb) * _sigmoid(-proj_scr[rows, C_BF:C_BF + LB])
        if valid < r:
            kb = jnp.where(row_ok_all if rows == slice(None) else row_ok, kb, 0.0)
        return kb

    gated_linear(lambda rows: _silu(proj_scr[rows, C_BQ:C_BQ + LB]), k_b,
                 lambda rows: proj_scr[rows, C_BI:C_BI + VB],
                 lambda rows, k=None: jnp.log1p(-(k_b(rows) if k is None else k)),
                 trib_ref, shgrn_scr, pb32_ref, pb16_ref, VA, cb)

    i_pre = proj_scr[:, C_SMALL:C_SMALL + LANES] + ibias_ref[...]
    lsf = _log_sigmoid(proj_scr[:, C_FGATE:C_FGATE + LANES] + fbias_ref[...])
    if valid < r:
        lsf = jnp.where(row_ok_all, lsf, 0.0)
        i_pre = jnp.where(row_ok_all, i_pre, NEG_BIG)
    f_cum = _tri_cumsum(tric_ref[...], lsf)
    a_all = i_pre - f_cum
    causal = (lax.broadcasted_iota(jnp.int32, (r, r), 0) >= lax.broadcasted_iota(jnp.int32, (r, r), 1))
    ones16 = jnp.ones((r, DV_C), BF16)

    def lanes(col):
        return jnp.broadcast_to(col, (col.shape[0], LANES))

    def head_cols(c0, b, h):
        return proj_scr[b * r:(b + 1) * r, c0 + h * DK_C:c0 + (h + 1) * DK_C]

    for b0 in range(0, nb, PAIR_GROUP // H):
        pairs = [(b, h) for b in range(b0, b0 + PAIR_GROUP // H) for h in range(H)]
        a_rows = {b: a_all[b * r:(b + 1) * r].T for b in range(b0, b0 + PAIR_GROUP // H)}
        f_rep = {(b, h): lanes(f_cum[b * r:(b + 1) * r, LANE_G + h:LANE_G + h + 1]) for b, h in pairs}
        a_rep = {(b, h): lanes(a_all[b * r:(b + 1) * r, LANE_G + h:LANE_G + h + 1]) for b, h in pairs}
        m_prev = {(b, h): mrep_scr[b, h:h + 1, :] for b, h in pairs}
        a_mat = {(b, h): jnp.where(causal, a_rows[b][LANE_G + h:LANE_G + h + 1, :], NEG_BIG) for b, h in pairs}
        g = {p: jnp.maximum(lanes(jnp.max(a_mat[p], axis=1, keepdims=True)), m_prev[p]) for p in pairs}
        d = {p: jnp.exp(a_mat[p] - g[p][:, 0:r]) for p in pairs}
        w_prev = {p: jnp.exp(m_prev[p] - g[p]) for p in pairs}
        m_t = {p: f_rep[p] + g[p] for p in pairs}
        q = {(b, h): head_cols(C_CQ, b, h) for b, h in pairs}
        q16 = {p: q[p].astype(BF16) for p in pairs}
        k = {(b, h): head_cols(C_CK, b, h) * DK_C ** -0.5 for b, h in pairs}
        v16 = {(b, h): head_cols(C_CV, b, h).astype(BF16) for b, h in pairs}
        if n_as_column:
            v16 = {p: jnp.concatenate([v16[p], ones16], axis=1) for p in pairs}
        qk = {p: _dot_nt(q16[p], k[p].astype(BF16)) * d[p] for p in pairs}
        c_prev = {(b, h): cst_scr[b, h] for b, h in pairs}
        now = {p: _dot(qk[p].astype(BF16), v16[p]) for p in pairs}
        old = {p: _dot(q16[p], c_prev[p].astype(BF16)) for p in pairs}
        for b, h in pairs:
            p = (b, h)
            if n_as_column:
                both = now[p] + jnp.concatenate([w_prev[p], w_prev[p]], axis=1) * old[p]
                num, den = both[:, 0:DV_C], both[:, DV_C:]
            else:
                num = now[p] + w_prev[p] * old[p]
                den = lanes(jnp.sum(qk[p], axis=1, keepdims=True)) + w_prev[p] * lanes(
                    jnp.sum(q[p] * nrow_scr[b, h:h + 1, :], axis=1, keepdims=True))
            mixed_scr[b * r:(b + 1) * r, VA + VB + h * DV_C:VA + VB + (h + 1) * DV_C] = (
                num / jnp.maximum(jnp.abs(den), jnp.exp(-m_t[p])))
        g_last = {p: g[p][r - 1:r, :] for p in pairs}
        w_old = {p: jnp.exp(m_prev[p] - g_last[p]) for p in pairs}
        kw = {p: k[p] * jnp.exp(a_rep[p] - g_last[p]) for p in pairs}
        upd = {p: _dot_tn(kw[p].astype(BF16), v16[p]) for p in pairs}
        for b, h in pairs:
            p = (b, h)
            if n_as_column:
                cst_scr[b, h] = jnp.concatenate([w_old[p], w_old[p]], axis=1) * c_prev[p] + upd[p]
            else:
                cst_scr[b, h] = w_old[p] * c_prev[p] + upd[p]
                nrow_scr[b, h:h + 1, :] = (w_old[p] * nrow_scr[b, h:h + 1, :]
                                           + jnp.sum(kw[p], axis=0, keepdims=True))
            mrep_scr[b, h:h + 1, :] = m_t[p][r - 1:r, :]

    def fold_states():
        for b in range(nb):
            for h in range(H):
                gla_ref[b, h] = _head_block(sgla_scr[b], h)
                hgrn_ref[b, h] = _head_block(shgrn_scr[b], h)
                c_ref[b, h] = cst_scr[b, h, :, 0:DV_C]
                if n_as_column:
                    n_ref[b, h:h + 1, :] = cst_scr[b, h, :, DV_C:].T[0:1, :]
                m_ref[b:b + 1, h:h + 1] = mrep_scr[b, h:h + 1, 0:1]
            if not n_as_column:
                n_ref[b] = nrow_scr[b]
        if layer:
            for prev, out in zip(earlier_refs, out_refs):
                out[0:layer] = prev[...]

    if has_init:
        fold_states()
    else:
        pl.when(step == n_steps - 1)(fold_states)

    o_a = mixed_scr[:, 0:VA]
    ms = _dot((o_a * o_a).astype(BF16), pb16_ref[...]) * (1.0 / DV_A)
    mixed_scr[:, 0:VA] = o_a * lax.rsqrt(ms + EPS) * gnorm_ref[...] * _silu(proj_scr[:, C_AG:C_AG + VA])
    o_b = mixed_scr[:, VA:VA + VB]
    ms = _dot((o_b * o_b).astype(BF16), pb16_ref[...]) * (1.0 / DV_B)
    mixed_scr[:, VA:VA + VB] = o_b * lax.rsqrt(ms + EPS) * hnorm_ref[...] * _silu(proj_scr[:, C_BG:C_BG + VB])
    for h in range(H):
        cols = slice(VA + VB + h * DV_C, VA + VB + (h + 1) * DV_C)
        gate = _sigmoid(proj_scr[:, C_CO + h * DV_C:C_CO + (h + 1) * DV_C])
        mixed_scr[:, cols] = gate * _rms(mixed_scr[:, cols], mnorm_ref[...])
    y = _dot(mixed_scr[...].astype(BF16), wout_ref[...])
    y = load_x() + _rms(y, gpost_ref[...])
    if valid == r:
        y_ref[...] = y.reshape(nb, r, D_MODEL)
    else:
        y_ref[...] = jnp.concatenate([y[b * r:b * r + valid] for b in range(nb)], axis=0)


def _block_tri(n, block):
    i, j = np.arange(n)[:, None], np.arange(n)[None, :]
    return jnp.asarray((j <= i) & (i // block == j // block), BF16)


def _mixer(layer, x, w, consts, init=None, earlier=None, *, nb, r):
    has_init = init is not None
    if has_init:
        batch = init[0].shape[1]
        valid = x.shape[0] // batch
        n_steps = batch // nb
        x_spec = pl.BlockSpec((nb * valid, D_MODEL), lambda i: (i, 0))
        b_idx = lambda i: i
    else:
        batch, tokens, _ = x.shape
        assert batch == nb
        valid = r
        n_steps = tokens // r
        x_spec = pl.BlockSpec((nb, r, D_MODEL), lambda i: (0, i, 0))
        b_idx = lambda i: 0
    rows_all = nb * r
    slab = min(MXU_TILE, rows_all)
    tris = [_block_tri(r, SUB), _block_tri(slab, min(CHUNK_A, r)), _block_tri(slab, min(CHUNK_B, r)),
            _block_tri(slab, r)]

    state_shapes = [(batch, H, DK_A, DV_A), (batch, H, E_B, DV_B), (batch, H, DK_C, DV_C), (batch, H, DK_C),
                    (batch, H)]

    def stacked_specs(n_layers):
        return [pl.BlockSpec((n_layers, nb) + s[1:],
                             (lambda nd: (lambda i: (0, b_idx(i)) + (0,) * (nd - 1)))(len(s))) for s in state_shapes]
    per_layer = [w["gains"], w["w_in"], w["w_lr"], w["b_lr"], w["gla_norm"]]
    per_layer2 = [w["hgrn_norm"], w["i_bias"], w["f_bias"], w["mlstm_norm"], w["w_out"]]
    shared = [consts["pa32"], consts["pa16"], consts["pb32"], consts["pb16"]] + tris
    in_arrays = [x] + per_layer + [w["lb_logits"]] + per_layer2 + shared
    in_specs = ([x_spec] + [_layer_spec(layer, a.shape[1:]) for a in per_layer]
                + [_const_spec(w["lb_logits"].shape)] + [_layer_spec(layer, a.shape[1:]) for a in per_layer2]
                + [_const_spec(a.shape) for a in shared])
    if has_init:
        in_arrays += list(init)
        in_specs += [pl.BlockSpec((None, nb) + s[1:], (lambda nd: (lambda i: (layer, i) + (0,) * (nd - 1)))(len(s)))
                     for s in state_shapes]
    if layer:
        in_arrays += list(earlier)
        in_specs += stacked_specs(layer)
    scratch = [
        pltpu.VMEM((rows_all, D_MODEL), BF16),
        pltpu.VMEM((rows_all, D_IN_PAD), F32),
        pltpu.VMEM((rows_all, D_MODEL), F32),
        pltpu.VMEM((nb, LA, VA), F32),
        pltpu.VMEM((nb, LB, VB), F32),
        pltpu.VMEM((nb, H, DK_C, DV_C if has_init else 2 * DV_C), F32),
        pltpu.VMEM((nb, H, DK_C), F32),
        pltpu.VMEM((nb, H, LANES), F32),
    ]
    state_block = nb * 4 * (LA * LANES + LB * LANES + H * DK_C * DV_C + SUBLANES * LANES + LANES)
    vmem = (2 * (D_MODEL * D_IN_PAD + D_MODEL * D_MODEL)
            + rows_all * (D_MODEL * (2 + 4 + 4 * 4) + D_IN_PAD * 4)
            + nb * 4 * (LA * VA + LB * VB + H * DK_C * 2 * DV_C + 2 * SUBLANES * LANES)
            + state_block * ((2 if has_init else 0) + 2 * layer + 2 * (layer + 1)))
    outs = pl.pallas_call(
        functools.partial(_mixer_kernel, layer, nb, r, valid, has_init, n_steps),
        name="token_mixer_sample" if has_init else "token_mixer_prompt",
        out_shape=([jax.ShapeDtypeStruct(x.shape, F32)]
                   + [jax.ShapeDtypeStruct((layer + 1,) + s, F32) for s in state_shapes]),
        grid=(n_steps,),
        in_specs=in_specs,
        out_specs=[x_spec] + stacked_specs(layer + 1),
        scratch_shapes=scratch,
        compiler_params=pltpu.CompilerParams(
            dimension_semantics=("arbitrary",),
            vmem_limit_bytes=min(VMEM_BYTES, vmem + 16 * 2**20),
        ),
    )(*in_arrays)
    return outs[0], outs[1:]


def _head_pattern(rows_per_head, cols_per_head):
    rr = np.arange(H * rows_per_head)[:, None] // rows_per_head
    cc = np.arange(H * cols_per_head)[None, :] // cols_per_head
    return (rr == cc).astype(np.float32)


def _constants():
    pa, pb = _head_pattern(DK_A, DV_A), _head_pattern(E_B, DV_B)
    return {"pa32": jnp.asarray(pa), "pa16": jnp.asarray(pa, BF16),
            "pb32": jnp.asarray(pb), "pb16": jnp.asarray(pb, BF16)}


def _mixer_weights(norm_gains, w_in, gla_w_lr, gla_b_lr, gla_norm, hgrn_lb_logits, hgrn_norm,
                   mlstm_i_bias, mlstm_f_bias, mlstm_norm, w_out):
    (a_q, a_k, a_v, a_g, a_lr, b_q, b_f, b_i, b_g, c_q, c_k, c_v, c_o, c_i, c_f) = jnp.split(
        w_in, np.cumsum(SPLIT_SIZES)[:-1].tolist(), axis=2)
    zeros = lambda n: jnp.zeros((DEPTH, D_MODEL, n), F32)
    packed = jnp.concatenate(
        [a_q, a_k, a_v, a_g, b_q, b_f, b_i, b_g, c_q, c_k, c_v, c_o,
         a_lr, c_i, zeros(LANES - RANK_A - H), zeros(LANE_G), c_f, zeros(D_IN_PAD - C_FGATE - LANE_G - H)], axis=2)
    rows = lambda v: v.reshape(DEPTH, 1, -1).astype(F32)
    gate_lanes = lambda v: rows(jnp.pad(v, ((0, 0), (LANE_G, LANES - LANE_G - H))))
    return {
        "gains": norm_gains.astype(F32),
        "w_in": packed.astype(BF16),
        "w_lr": jnp.pad(gla_w_lr, ((0, 0), (0, LANES - RANK_A), (0, 0))).astype(BF16),
        "b_lr": rows(gla_b_lr),
        "gla_norm": rows(jnp.tile(gla_norm, (1, H))),
        "lb_logits": hgrn_lb_logits.astype(F32),
        "hgrn_norm": rows(jnp.tile(hgrn_norm, (1, H))),
        "i_bias": gate_lanes(mlstm_i_bias), "f_bias": gate_lanes(mlstm_f_bias),
        "mlstm_norm": rows(mlstm_norm),
        "w_out": w_out.astype(BF16),
    }


PROMPT_CHUNK = 64
SAMPLE_ROWS = SUBLANES
SAMPLE_GROUP = 8


def kernel(x_prompt, x_sample, state_gla, state_hgrn, state_mlstm_C, state_mlstm_n, state_mlstm_m, norm_gains, ffn1_w_gate, ffn1_w_up, ffn1_w_down, w_in, gla_w_lr, gla_b_lr, gla_norm, hgrn_lb_logits, hgrn_norm, mlstm_i_bias, mlstm_f_bias, mlstm_norm, w_out, ffn2_w_gate, ffn2_w_up, ffn2_w_down):
    bp, tp, _ = x_prompt.shape
    bs, ts, _ = x_sample.shape
    consts = _constants()
    yp = x_prompt.reshape(bp * tp, D_MODEL)
    ys = x_sample.reshape(bs * ts, D_MODEL)
    st_p = st_s = None
    mix = _mixer_weights(norm_gains, w_in, gla_w_lr, gla_b_lr, gla_norm, hgrn_lb_logits, hgrn_norm,
                         mlstm_i_bias, mlstm_f_bias, mlstm_norm, w_out)
    gains = mix["gains"]
    ffn1 = tuple(w.astype(BF16) for w in (ffn1_w_gate, ffn1_w_up, ffn1_w_down))
    ffn2 = tuple(w.astype(BF16) for w in (ffn2_w_gate, ffn2_w_up, ffn2_w_down))
    init = (state_gla, state_hgrn, state_mlstm_C, state_mlstm_n, state_mlstm_m)
    for l in range(DEPTH):
        yp = _ffn(l, 0, yp, gains, *ffn1)
        ys = _ffn(l, 0, ys, gains, *ffn1)
        yp3, st_p = _mixer(l, yp.reshape(bp, tp, D_MODEL), mix, consts, None, st_p, nb=bp, r=PROMPT_CHUNK)
        ys, st_s = _mixer(l, ys, mix, consts, init, st_s, nb=SAMPLE_GROUP, r=SAMPLE_ROWS)
        yp = _ffn(l, 4, yp3.reshape(bp * tp, D_MODEL), gains, *ffn2)
        ys = _ffn(l, 4, ys, gains, *ffn2)

    return (yp.reshape(bp, tp, D_MODEL), ys.reshape(bs, ts, D_MODEL)) + tuple(st_p) + tuple(st_s)
```

```python
import functools

import jax
import jax.numpy as jnp
import numpy as np
from jax import lax
from jax.experimental import pallas as pl
from jax.experimental.pallas import tpu as pltpu

F32 = jnp.float32
BF16 = jnp.bfloat16

D_MODEL = 1024
D_FF = 2688
DEPTH = 2
H = 4
DK_A, DV_A, RANK_A, TAU_A = 32, 64, 16, 16.0
E_B, DV_B = 64, 64
DK_C, DV_C = 128, 128
EPS = 1e-6
NEG_BIG = -1e30
M_INIT = -1e30
SPLIT_SIZES = (
    H * DK_A, H * DK_A, H * DV_A, H * DV_A, RANK_A,
    H * E_B, H * E_B, H * DV_B, H * DV_B,
    H * DK_C, H * DK_C, H * DV_C, H * DV_C, H, H,
)

LANES = 128
SUBLANES = 8
MXU_TILE = 256
VMEM_BYTES = 64 * 2**20

SUB = SUBLANES
LA, VA = H * DK_A, H * DV_A
LB, VB = H * E_B, H * DV_B
N_FF_TILES, FF_TAIL = divmod(D_FF, MXU_TILE)
FF_TILES_PER_ITER = 2
TOKEN_TILE = 1024

C_AQ, C_AK, C_AV, C_AG = 0, LA, 2 * LA, 2 * LA + VA
C_BQ = C_AG + VA
C_BF, C_BI, C_BG = C_BQ + LB, C_BQ + 2 * LB, C_BQ + 3 * LB
C_CQ = C_BG + VB
C_CK, C_CV, C_CO = C_CQ + H * DK_C, C_CQ + 2 * H * DK_C, C_CQ + 2 * H * DK_C + H * DV_C
C_SMALL = C_CO + H * DV_C
C_FGATE = C_SMALL + LANES
N_IN_TILES = pl.cdiv(C_FGATE + LANES, MXU_TILE)
D_IN_PAD = N_IN_TILES * MXU_TILE
IN_TILES_PER_ITER = 8
LANE_G = RANK_A

CHUNK_A, CHUNK_B = 64, 64
PAIR_GROUP = 8
FAST_DECAY_LIMIT = 60.0


def _rms(x, g):
    return x * lax.rsqrt(jnp.mean(x * x, axis=-1, keepdims=True) + EPS) * g


def _sigmoid(x):
    return 1.0 / (1.0 + jnp.exp(-x))


def _silu(x):
    return x * _sigmoid(x)


def _log_sigmoid(x):
    return jnp.minimum(x, 0.0) - jnp.log1p(jnp.exp(-jnp.abs(x)))


def _dot(a, b):
    return jnp.dot(a, b, preferred_element_type=F32)


def _dot_tn(a, b):
    return lax.dot_general(a, b, (((0,), (0,)), ((), ())), preferred_element_type=F32)


def _dot_nt(a, b):
    return lax.dot_general(a, b, (((1,), (1,)), ((), ())), preferred_element_type=F32)


def _tri_cumsum(tri16, x):
    t = tri16.shape[0]
    hi = x.astype(BF16)
    r1 = x - hi.astype(F32)
    mid = r1.astype(BF16)
    lo = (r1 - mid.astype(F32)).astype(BF16)
    out = []
    for s in range(0, x.shape[0], t):
        out.append(_dot(tri16, hi[s:s + t]) + _dot(tri16, mid[s:s + t]) + _dot(tri16, lo[s:s + t]))
    return out[0] if len(out) == 1 else jnp.concatenate(out, axis=0)


def _pad_rows(x, mult):
    pad = -x.shape[0] % mult
    return x if pad == 0 else jnp.concatenate([x, jnp.zeros((pad, x.shape[1]), x.dtype)], axis=0)


def _ffn_kernel(gain_row, n_main, x_ref, xs_ref, gains_ref, wg_ref, wu_ref, wd_ref, o_ref, os_ref, h_scr, acc_scr):
    gpre_ref, gpost_ref = gains_ref.at[gain_row:gain_row + 1, :], gains_ref.at[gain_row + 1:gain_row + 2, :]

    def token_tile(xin_ref, out_ref):
        rows = xin_ref.shape[0]
        h_ref, acc_ref = h_scr.at[0:rows, :], acc_scr.at[0:rows, :]
        h_ref[...] = _rms(xin_ref[...], gpre_ref[...]).astype(BF16)

        def ff_cols(cols):
            h = h_ref[...]
            g = _dot(h, wg_ref[:, cols])
            u = _dot(h, wu_ref[:, cols])
            return _dot((_silu(g) * u).astype(BF16), wd_ref[cols, :])

        def ff_tiles(j, carry):
            for u in range(FF_TILES_PER_ITER):
                acc_ref[...] += ff_cols(
                    pl.ds(pl.multiple_of((j * FF_TILES_PER_ITER + u) * MXU_TILE, MXU_TILE), MXU_TILE))
            return carry

        acc_ref[...] = ff_cols(pl.ds(N_FF_TILES * MXU_TILE, FF_TAIL))
        lax.fori_loop(0, N_FF_TILES // FF_TILES_PER_ITER, ff_tiles, 0)
        out_ref[...] = xin_ref[...] + _rms(acc_ref[...], 0.5 * gpost_ref[...])

    step = pl.program_id(0)
    pl.when(step < n_main)(lambda: token_tile(x_ref, o_ref))
    pl.when(step == n_main)(lambda: token_tile(xs_ref, os_ref))


def _const_spec(shape):
    return pl.BlockSpec(shape, lambda *_: (0,) * len(shape), pipeline_mode=pl.Buffered(1))


def _layer_spec(layer, shape):
    return pl.BlockSpec((None,) + tuple(shape), lambda *_: (layer,) + (0,) * len(shape),
                        pipeline_mode=pl.Buffered(1))


def _ffn(layer, gain_row, x, xs, gains, wg, wu, wd):
    n, ns = x.shape[0], xs.shape[0]
    tm = TOKEN_TILE
    n_main = n // tm
    assert n_main * tm == n and ns <= tm
    main_map = lambda i: (jnp.minimum(i, n_main - 1), 0)
    weights = 3 * D_MODEL * D_FF * 2
    tiles = (tm + ns) * D_MODEL * 2 * 2 * 4 + tm * D_MODEL * (2 + 4) + 4 * tm * MXU_TILE * 4
    return pl.pallas_call(
        functools.partial(_ffn_kernel, gain_row, n_main),
        name="swiglu_half_step",
        out_shape=[jax.ShapeDtypeStruct((n, D_MODEL), F32), jax.ShapeDtypeStruct((ns, D_MODEL), F32)],
        grid=(n_main + 1,),
        in_specs=[
            pl.BlockSpec((tm, D_MODEL), main_map),
            pl.BlockSpec((ns, D_MODEL), lambda i: (0, 0)),
            _layer_spec(layer, gains.shape[1:]),
            _layer_spec(layer, (D_MODEL, D_FF)),
            _layer_spec(layer, (D_MODEL, D_FF)),
            _layer_spec(layer, (D_FF, D_MODEL)),
        ],
        out_specs=[pl.BlockSpec((tm, D_MODEL), main_map), pl.BlockSpec((ns, D_MODEL), lambda i: (0, 0))],
        scratch_shapes=[pltpu.VMEM((tm, D_MODEL), BF16), pltpu.VMEM((tm, D_MODEL), F32)],
        compiler_params=pltpu.CompilerParams(
            dimension_semantics=("arbitrary",),
            vmem_limit_bytes=min(VMEM_BYTES, weights + tiles + 16 * 2**20),
        ),
    )(x, xs, gains, wg, wu, wd)


def _gated_linear_exact(q, k, v, la, s_prev, pat32, pat16, tri_sub16):
    r, l = q.shape
    vv = v.shape[1]
    n_sub = r // SUB
    b = _tri_cumsum(tri_sub16, la)
    b3, q3, k3, v3 = (a.reshape(n_sub, SUB, a.shape[1]) for a in (b, q, k, v))
    bend3 = b3[:, SUB - 1:SUB, :]
    qh3 = q3 * jnp.exp(b3)
    kh3 = k3 * jnp.exp(bend3 - b3)

    tpos = lax.broadcasted_iota(jnp.int32, (n_sub, SUB, l), 1)
    intra3 = jnp.zeros((n_sub, SUB, vv), F32)
    for j in range(SUB):
        arg = jnp.where(tpos >= j, b3 - b3[:, j:j + 1, :], NEG_BIG)
        p = (q3 * k3[:, j:j + 1, :] * jnp.exp(arg)).reshape(r, l)
        sc = _dot(p.astype(BF16), pat16)
        intra3 = intra3 + sc.reshape(n_sub, SUB, vv) * v3[:, j:j + 1, :]

    dcols = jnp.exp(_pad_rows(bend3.reshape(n_sub, l), SUBLANES)).T
    s = s_prev
    inter = []
    for i in range(n_sub):
        inter.append(_dot(qh3[i].astype(BF16), s.astype(BF16)))
        u = _dot_tn(kh3[i].astype(BF16), v3[i].astype(BF16))
        s = s * dcols[:, i:i + 1] + u * pat32
    o = inter[0] if n_sub == 1 else jnp.concatenate(inter, axis=0)
    return o + intra3.reshape(r, vv), s


def _gated_linear_fast(q, k, v, bc, s_scr, o_store, pat32, nb, r, c):
    l, vv = q.shape[1], v.shape[1]
    n_ch = r // c
    n_tot = nb * n_ch
    bc3, q3, k3, v3 = (a.reshape(n_tot, c, a.shape[1]) for a in (bc, q, k, v))
    bmid = bc3[:, c // 2 - 1:c // 2, :]
    blast = bc3[:, c - 1:c, :]
    qt = (q3 * jnp.exp(bc3 - bmid)).astype(BF16)
    kt = (k3 * jnp.exp(bmid - bc3)).astype(BF16)
    qc = (q3 * jnp.exp(bc3)).astype(BF16)
    ke = (k3 * jnp.exp(blast - bc3)).astype(BF16)
    v16 = v3.astype(BF16)
    dcols = jnp.exp(_pad_rows(blast.reshape(n_tot, l), SUBLANES)).T

    head_l = lax.broadcasted_iota(jnp.int32, (c, l), 1) // (l // H)
    head_v = lax.broadcasted_iota(jnp.int32, (c, vv), 1) // (vv // H)
    causal = (lax.broadcasted_iota(jnp.int32, (H * c, c), 0) % c >= lax.broadcasted_iota(jnp.int32, (H * c, c), 1))
    zero16 = jnp.zeros((), BF16)

    idxs = range(n_tot)
    lhs = [jnp.concatenate([jnp.where(head_l == h, qt[i], zero16) for h in range(H)], axis=0) for i in idxs]
    sc = [_dot_nt(lhs[i], kt[i]) for i in idxs]
    sc = [jnp.where(causal, sc[i], 0.0).astype(BF16) for i in idxs]
    ov = [_dot(sc[i], v16[i]) for i in idxs]
    intra = [functools.reduce(jnp.add, [jnp.where(head_v == h, ov[i][h * c:(h + 1) * c], 0.0) for h in range(H)])
             for i in idxs]
    u = [_dot_tn(ke[i], v16[i]) for i in idxs]

    for ci in range(n_ch):
        ids = [b * n_ch + ci for b in range(nb)]
        s_old = [s_scr[b] for b in range(nb)]
        inter = [_dot(qc[i], s_old[b].astype(BF16)) for b, i in enumerate(ids)]
        for b, i in enumerate(ids):
            o_store(b, ci * c, c, intra[i] + inter[b])
            s_scr[b] = s_old[b] * dcols[:, i:i + 1] + u[i] * pat32


def _expand_heads(per_head):
    dk, dv = per_head[0].shape
    rows = []
    for h in range(H):
        parts = [jnp.zeros((dk, h * dv), F32), per_head[h], jnp.zeros((dk, (H - 1 - h) * dv), F32)]
        rows.append(jnp.concatenate([p for p in parts if p.shape[1]], axis=1))
    return jnp.concatenate(rows, axis=0)


def _head_block(s, h):
    dk, dv = s.shape[0] // H, s.shape[1] // H
    return s[h * dk:(h + 1) * dk, h * dv:(h + 1) * dv]


def _mixer_kernel(layer, nb, r, valid, has_init, n_steps, *refs):
    (x_ref, gains_ref, win_ref, wlr_ref, blr_ref, gnorm_ref, lb_ref, hnorm_ref,
     ibias_ref, fbias_ref, mnorm_ref, wout_ref, pa32_ref, pa16_ref, pb32_ref, pb16_ref, trisub_ref,
     tria_ref, trib_ref, tric_ref) = refs[:20]
    refs = refs[20:]
    gpre_ref, gpost_ref = gains_ref.at[2:3, :], gains_ref.at[3:4, :]
    if has_init:
        gla0_ref, hgrn0_ref, c0_ref, n0_ref, m0_ref = refs[:5]
        refs = refs[5:]
    if layer:
        earlier_refs = refs[:5]
        refs = refs[5:]
    y_ref = refs[0]
    out_refs = refs[1:6]
    gla_ref, hgrn_ref, c_ref, n_ref, m_ref = (o.at[layer] for o in out_refs)
    hin_scr, proj_scr, mixed_scr, sgla_scr, shgrn_scr, cst_scr, nrow_scr, mrep_scr = refs[6:]
    n_as_column = not has_init
    rows_all = nb * r
    ca, cb = min(CHUNK_A, r), min(CHUNK_B, r)
    step = pl.program_id(0)

    if has_init:
        for b in range(nb):
            sgla_scr[b] = _expand_heads([gla0_ref[b, h] for h in range(H)])
            shgrn_scr[b] = _expand_heads([hgrn0_ref[b, h] for h in range(H)])
            for h in range(H):
                mrep_scr[b, h:h + 1, :] = jnp.broadcast_to(m0_ref[b:b + 1, h:h + 1], (1, LANES))
        cst_scr[...] = c0_ref[...]
        nrow_scr[...] = n0_ref[...]
    else:
        @pl.when(step == 0)
        def _():
            sgla_scr[...] = jnp.zeros_like(sgla_scr)
            shgrn_scr[...] = jnp.zeros_like(shgrn_scr)
            cst_scr[...] = jnp.zeros_like(cst_scr)
            mrep_scr[...] = jnp.full(mrep_scr.shape, M_INIT, F32)

    def load_x():
        if valid == r:
            return x_ref[...].reshape(rows_all, D_MODEL)
        x_in = x_ref[...]
        pad = jnp.zeros((r - valid, D_MODEL), F32)
        return jnp.concatenate([p for b in range(nb) for p in (x_in[b * valid:(b + 1) * valid], pad)], axis=0)

    hin_scr[...] = _rms(load_x(), gpre_ref[...]).astype(BF16)

    def in_tiles(t, carry):
        for u in range(IN_TILES_PER_ITER):
            cols = pl.ds(pl.multiple_of((t * IN_TILES_PER_ITER + u) * MXU_TILE, MXU_TILE), MXU_TILE)
            proj_scr[:, cols] = _dot(hin_scr[...], win_ref[:, cols])
        return carry

    lax.fori_loop(0, N_IN_TILES // IN_TILES_PER_ITER, in_tiles, 0)

    lg = [lb_ref[j:j + 1, :] for j in range(DEPTH)]
    lg_max = functools.reduce(jnp.maximum, lg)
    ex = [jnp.exp(v - lg_max) for v in lg]
    ex_sum = functools.reduce(jnp.add, ex)
    lb = functools.reduce(jnp.add, [e / ex_sum for e in ex[1:layer + 1]], jnp.zeros_like(lg_max))

    if valid < r:
        row_ok_all = lax.broadcasted_iota(jnp.int32, (rows_all, 1), 0) % r < valid
        row_ok = row_ok_all[0:r]

    def store_cols(c0):
        def store(b, t0, n, val):
            mixed_scr[b * r + t0:b * r + t0 + n, c0:c0 + val.shape[1]] = val
        return store

    def exact_path(q_of, k_of, v_of, la_of, s_scr, pat32_ref, pat16_ref, c0):
        def one(bi, carry):
            rows = pl.ds(pl.multiple_of(bi * r, r), r)
            o, s1 = _gated_linear_exact(q_of(rows), k_of(rows), v_of(rows), la_of(rows), s_scr[bi],
                                        pat32_ref[...], pat16_ref[...], trisub_ref[...])
            mixed_scr[rows, c0:c0 + o.shape[1]] = o
            s_scr[bi] = s1
            return carry
        lax.fori_loop(0, nb, one, 0)

    def gated_linear(q_of, k_of, v_of, la_of, tri_ref, s_scr, pat32_ref, pat16_ref, c0, c):
        everything = slice(None)
        k_all = k_of(everything)
        bc = _tri_cumsum(tri_ref[...], la_of(everything, k_all))
        bc3 = bc.reshape(rows_all // c, c, bc.shape[1])
        to_mid, past_mid = -bc3[:, c // 2 - 1, :], bc3[:, c // 2 - 1, :] - bc3[:, c - 1, :]
        safe = jnp.max(jnp.maximum(to_mid, past_mid)) <= FAST_DECAY_LIMIT

        @pl.when(safe)
        def _():
            _gated_linear_fast(q_of(everything), k_all, v_of(everything), bc, s_scr,
                               store_cols(c0), pat32_ref[...], nb, r, c)

        @pl.when(jnp.logical_not(safe))
        def _():
            exact_path(q_of, k_of, v_of, la_of, s_scr, pat32_ref, pat16_ref, c0)

    def la_a(rows, k=None):
        small = proj_scr[rows, C_SMALL:C_SMALL + LANES]
        la = _log_sigmoid(_dot(small.astype(BF16), wlr_ref[...]) + blr_ref[...]) * (1.0 / TAU_A)
        if valid < r:
            la = jnp.where(row_ok_all if rows == slice(None) else row_ok, la, 0.0)
        return la

    gated_linear(lambda rows: proj_scr[rows, C_AQ:C_AQ + LA] * DK_A ** -0.5,
                 lambda rows: proj_scr[rows, C_AK:C_AK + LA],
                 lambda rows: proj_scr[rows, C_AV:C_AV + VA],
                 la_a, tria_ref, sgla_scr, pa32_ref, pa16_ref, 0, ca)

    def k_b(rows):
        kb = (1.0 - lb) * _sigmoid(-proj_scr[rows, C_BF:C_BF + LB])
        if valid < r:
            kb = jnp.where(row_ok_all if rows == slice(None) else row_ok, kb, 0.0)
        return kb

    gated_linear(lambda rows: _silu(proj_scr[rows, C_BQ:C_BQ + LB]), k_b,
                 lambda rows: proj_scr[rows, C_BI:C_BI + VB],
                 lambda rows, k=None: jnp.log1p(-(k_b(rows) if k is None else k)),
                 trib_ref, shgrn_scr, pb32_ref, pb16_ref, VA, cb)

    i_pre = proj_scr[:, C_SMALL:C_SMALL + LANES] + ibias_ref[...]
    lsf = _log_sigmoid(proj_scr[:, C_FGATE:C_FGATE + LANES] + fbias_ref[...])
    if valid < r:
        lsf = jnp.where(row_ok_all, lsf, 0.0)
        i_pre = jnp.where(row_ok_all, i_pre, NEG_BIG)
    f_cum = _tri_cumsum(tric_ref[...], lsf)
    a_all = i_pre - f_cum
    causal = (lax.broadcasted_iota(jnp.int32, (r, r), 0) >= lax.broadcasted_iota(jnp.int32, (r, r), 1))
    ones16 = jnp.ones((r, DV_C), BF16)

    def lanes(col):
        return jnp.broadcast_to(col, (col.shape[0], LANES))

    def head_cols(c0, b, h):
        return proj_scr[b * r:(b + 1) * r, c0 + h * DK_C:c0 + (h + 1) * DK_C]

    for b0 in range(0, nb, PAIR_GROUP // H):
        pairs = [(b, h) for b in range(b0, b0 + PAIR_GROUP // H) for h in range(H)]
        a_rows = {b: a_all[b * r:(b + 1) * r].T for b in range(b0, b0 + PAIR_GROUP // H)}
        f_rep = {(b, h): lanes(f_cum[b * r:(b + 1) * r, LANE_G + h:LANE_G + h + 1]) for b, h in pairs}
        a_rep = {(b, h): lanes(a_all[b * r:(b + 1) * r, LANE_G + h:LANE_G + h + 1]) for b, h in pairs}
        m_prev = {(b, h): mrep_scr[b, h:h + 1, :] for b, h in pairs}
        a_mat = {(b, h): jnp.where(causal, a_rows[b][LANE_G + h:LANE_G + h + 1, :], NEG_BIG) for b, h in pairs}
        g = {p: jnp.maximum(lanes(jnp.max(a_mat[p], axis=1, keepdims=True)), m_prev[p]) for p in pairs}
        d = {p: jnp.exp(a_mat[p] - g[p][:, 0:r]) for p in pairs}
        w_prev = {p: jnp.exp(m_prev[p] - g[p]) for p in pairs}
        m_t = {p: f_rep[p] + g[p] for p in pairs}
        q = {(b, h): head_cols(C_CQ, b, h) for b, h in pairs}
        q16 = {p: q[p].astype(BF16) for p in pairs}
        k = {(b, h): head_cols(C_CK, b, h) * DK_C ** -0.5 for b, h in pairs}
        v16 = {(b, h): head_cols(C_CV, b, h).astype(BF16) for b, h in pairs}
        if n_as_column:
            v16 = {p: jnp.concatenate([v16[p], ones16], axis=1) for p in pairs}
        qk = {p: _dot_nt(q16[p], k[p].astype(BF16)) * d[p] for p in pairs}
        c_prev = {(b, h): cst_scr[b, h] for b, h in pairs}
        now = {p: _dot(qk[p].astype(BF16), v16[p]) for p in pairs}
        old = {p: _dot(q16[p], c_prev[p].astype(BF16)) for p in pairs}
        for b, h in pairs:
            p = (b, h)
            if n_as_column:
                both = now[p] + jnp.concatenate([w_prev[p], w_prev[p]], axis=1) * old[p]
                num, den = both[:, 0:DV_C], both[:, DV_C:]
            else:
                num = now[p] + w_prev[p] * old[p]
                den = lanes(jnp.sum(qk[p], axis=1, keepdims=True)) + w_prev[p] * lanes(
                    jnp.sum(q[p] * nrow_scr[b, h:h + 1, :], axis=1, keepdims=True))
            mixed_scr[b * r:(b + 1) * r, VA + VB + h * DV_C:VA + VB + (h + 1) * DV_C] = (
                num / jnp.maximum(jnp.abs(den), jnp.exp(-m_t[p])))
        g_last = {p: g[p][r - 1:r, :] for p in pairs}
        w_old = {p: jnp.exp(m_prev[p] - g_last[p]) for p in pairs}
        kw = {p: k[p] * jnp.exp(a_rep[p] - g_last[p]) for p in pairs}
        upd = {p: _dot_tn(kw[p].astype(BF16), v16[p]) for p in pairs}
        for b, h in pairs:
            p = (b, h)
            if n_as_column:
                cst_scr[b, h] = jnp.concatenate([w_old[p], w_old[p]], axis=1) * c_prev[p] + upd[p]
            else:
                cst_scr[b, h] = w_old[p] * c_prev[p] + upd[p]
                nrow_scr[b, h:h + 1, :] = (w_old[p] * nrow_scr[b, h:h + 1, :]
                                           + jnp.sum(kw[p], axis=0, keepdims=True))
            mrep_scr[b, h:h + 1, :] = m_t[p][r - 1:r, :]

    def fold_states():
        for b in range(nb):
            for h in range(H):
                gla_ref[b, h] = _head_block(sgla_scr[b], h)
                hgrn_ref[b, h] = _head_block(shgrn_scr[b], h)
                c_ref[b, h] = cst_scr[b, h, :, 0:DV_C]
                if n_as_column:
                    n_ref[b, h:h + 1, :] = cst_scr[b, h, :, DV_C:].T[0:1, :]
                m_ref[b:b + 1, h:h + 1] = mrep_scr[b, h:h + 1, 0:1]
            if not n_as_column:
                n_ref[b] = nrow_scr[b]
        if layer:
            for prev, out in zip(earlier_refs, out_refs):
                out[0:layer] = prev[...]

    if has_init:
        fold_states()
    else:
        pl.when(step == n_steps - 1)(fold_states)

    o_a = mixed_scr[:, 0:VA]
    ms = _dot((o_a * o_a).astype(BF16), pb16_ref[...]) * (1.0 / DV_A)
    mixed_scr[:, 0:VA] = o_a * lax.rsqrt(ms + EPS) * gnorm_ref[...] * _silu(proj_scr[:, C_AG:C_AG + VA])
    o_b = mixed_scr[:, VA:VA + VB]
    ms = _dot((o_b * o_b).astype(BF16), pb16_ref[...]) * (1.0 / DV_B)
    mixed_scr[:, VA:VA + VB] = o_b * lax.rsqrt(ms + EPS) * hnorm_ref[...] * _silu(proj_scr[:, C_BG:C_BG + VB])
    for h in range(H):
        cols = slice(VA + VB + h * DV_C, VA + VB + (h + 1) * DV_C)
        gate = _sigmoid(proj_scr[:, C_CO + h * DV_C:C_CO + (h + 1) * DV_C])
        mixed_scr[:, cols] = gate * _rms(mixed_scr[:, cols], mnorm_ref[...])
    y = _dot(mixed_scr[...].astype(BF16), wout_ref[...])
    y = load_x() + _rms(y, gpost_ref[...])
    if valid == r:
        y_ref[...] = y.reshape(nb, r, D_MODEL)
    else:
        y_ref[...] = jnp.concatenate([y[b * r:b * r + valid] for b in range(nb)], axis=0)


def _block_tri(n, block):
    i, j = np.arange(n)[:, None], np.arange(n)[None, :]
    return jnp.asarray((j <= i) & (i // block == j // block), BF16)


def _mixer(layer, x, w, consts, init=None, earlier=None, *, nb, r):
    has_init = init is not None
    if has_init:
        batch = init[0].shape[1]
        valid = x.shape[0] // batch
        n_steps = batch // nb
        x_spec = pl.BlockSpec((nb * valid, D_MODEL), lambda i: (i, 0))
        b_idx = lambda i: i
    else:
        batch, tokens, _ = x.shape
        assert batch == nb
        valid = r
        n_steps = tokens // r
        x_spec = pl.BlockSpec((nb, r, D_MODEL), lambda i: (0, i, 0))
        b_idx = lambda i: 0
    rows_all = nb * r
    slab = min(MXU_TILE, rows_all)
    tris = [_block_tri(r, SUB), _block_tri(slab, min(CHUNK_A, r)), _block_tri(slab, min(CHUNK_B, r)),
            _block_tri(slab, r)]

    state_shapes = [(batch, H, DK_A, DV_A), (batch, H, E_B, DV_B), (batch, H, DK_C, DV_C), (batch, H, DK_C),
                    (batch, H)]

    def stacked_specs(n_layers):
        return [pl.BlockSpec((n_layers, nb) + s[1:],
                             (lambda nd: (lambda i: (0, b_idx(i)) + (0,) * (nd - 1)))(len(s))) for s in state_shapes]
    per_layer = [w["gains"], w["w_in"], w["w_lr"], w["b_lr"], w["gla_norm"]]
    per_layer2 = [w["hgrn_norm"], w["i_bias"], w["f_bias"], w["mlstm_norm"], w["w_out"]]
    shared = [consts["pa32"], consts["pa16"], consts["pb32"], consts["pb16"]] + tris
    in_arrays = [x] + per_layer + [w["lb_logits"]] + per_layer2 + shared
    in_specs = ([x_spec] + [_layer_spec(layer, a.shape[1:]) for a in per_layer]
                + [_const_spec(w["lb_logits"].shape)] + [_layer_spec(layer, a.shape[1:]) for a in per_layer2]
                + [_const_spec(a.shape) for a in shared])
    if has_init:
        in_arrays += list(init)
        in_specs += [pl.BlockSpec((None, nb) + s[1:], (lambda nd: (lambda i: (layer, i) + (0,) * (nd - 1)))(len(s)))
                     for s in state_shapes]
    if layer:
        in_arrays += list(earlier)
        in_specs += stacked_specs(layer)
    scratch = [
        pltpu.VMEM((rows_all, D_MODEL), BF16),
        pltpu.VMEM((rows_all, D_IN_PAD), F32),
        pltpu.VMEM((rows_all, D_MODEL), F32),
        pltpu.VMEM((nb, LA, VA), F32),
        pltpu.VMEM((nb, LB, VB), F32),
        pltpu.VMEM((nb, H, DK_C, DV_C if has_init else 2 * DV_C), F32),
        pltpu.VMEM((nb, H, DK_C), F32),
        pltpu.VMEM((nb, H, LANES), F32),
    ]
    state_block = nb * 4 * (LA * LANES + LB * LANES + H * DK_C * DV_C + SUBLANES * LANES + LANES)
    vmem = (2 * (D_MODEL * D_IN_PAD + D_MODEL * D_MODEL)
            + rows_all * (D_MODEL * (2 + 4 + 4 * 4) + D_IN_PAD * 4)
            + nb * 4 * (LA * VA + LB * VB + H * DK_C * 2 * DV_C + 2 * SUBLANES * LANES)
            + state_block * ((2 if has_init else 0) + 2 * layer + 2 * (layer + 1)))
    outs = pl.pallas_call(
        functools.partial(_mixer_kernel, layer, nb, r, valid, has_init, n_steps),
        name="token_mixer_sample" if has_init else "token_mixer_prompt",
        out_shape=([jax.ShapeDtypeStruct(x.shape, F32)]
                   + [jax.ShapeDtypeStruct((layer + 1,) + s, F32) for s in state_shapes]),
        grid=(n_steps,),
        in_specs=in_specs,
        out_specs=[x_spec] + stacked_specs(layer + 1),
        scratch_shapes=scratch,
        compiler_params=pltpu.CompilerParams(
            dimension_semantics=("arbitrary",),
            vmem_limit_bytes=min(VMEM_BYTES, vmem + 16 * 2**20),
        ),
    )(*in_arrays)
    return outs[0], outs[1:]


def _head_pattern(rows_per_head, cols_per_head):
    rr = np.arange(H * rows_per_head)[:, None] // rows_per_head
    cc = np.arange(H * cols_per_head)[None, :] // cols_per_head
    return (rr == cc).astype(np.float32)


def _constants():
    pa, pb = _head_pattern(DK_A, DV_A), _head_pattern(E_B, DV_B)
    return {"pa32": jnp.asarray(pa), "pa16": jnp.asarray(pa, BF16),
            "pb32": jnp.asarray(pb), "pb16": jnp.asarray(pb, BF16)}


def _mixer_weights(norm_gains, w_in, gla_w_lr, gla_b_lr, gla_norm, hgrn_lb_logits, hgrn_norm,
                   mlstm_i_bias, mlstm_f_bias, mlstm_norm, w_out):
    (a_q, a_k, a_v, a_g, a_lr, b_q, b_f, b_i, b_g, c_q, c_k, c_v, c_o, c_i, c_f) = jnp.split(
        w_in, np.cumsum(SPLIT_SIZES)[:-1].tolist(), axis=2)
    zeros = lambda n: jnp.zeros((DEPTH, D_MODEL, n), F32)
    packed = jnp.concatenate(
        [a_q, a_k, a_v, a_g, b_q, b_f, b_i, b_g, c_q, c_k, c_v, c_o,
         a_lr, c_i, zeros(LANES - RANK_A - H), zeros(LANE_G), c_f, zeros(D_IN_PAD - C_FGATE - LANE_G - H)], axis=2)
    rows = lambda v: v.reshape(DEPTH, 1, -1).astype(F32)
    gate_lanes = lambda v: rows(jnp.pad(v, ((0, 0), (LANE_G, LANES - LANE_G - H))))
    return {
        "gains": norm_gains.astype(F32),
        "w_in": packed.astype(BF16),
        "w_lr": jnp.pad(gla_w_lr, ((0, 0), (0, LANES - RANK_A), (0, 0))).astype(BF16),
        "b_lr": rows(gla_b_lr),
        "gla_norm": rows(jnp.tile(gla_norm, (1, H))),
        "lb_logits": hgrn_lb_logits.astype(F32),
        "hgrn_norm": rows(jnp.tile(hgrn_norm, (1, H))),
        "i_bias": gate_lanes(mlstm_i_bias), "f_bias": gate_lanes(mlstm_f_bias),
        "mlstm_norm": rows(mlstm_norm),
        "w_out": w_out.astype(BF16),
    }


PROMPT_CHUNK = 64
SAMPLE_ROWS = SUBLANES
SAMPLE_GROUP = 8


def kernel(x_prompt, x_sample, state_gla, state_hgrn, state_mlstm_C, state_mlstm_n, state_mlstm_m, norm_gains, ffn1_w_gate, ffn1_w_up, ffn1_w_down, w_in, gla_w_lr, gla_b_lr, gla_norm, hgrn_lb_logits, hgrn_norm, mlstm_i_bias, mlstm_f_bias, mlstm_norm, w_out, ffn2_w_gate, ffn2_w_up, ffn2_w_down):
    bp, tp, _ = x_prompt.shape
    bs, ts, _ = x_sample.shape
    consts = _constants()
    yp = x_prompt.reshape(bp * tp, D_MODEL)
    ys = x_sample.reshape(bs * ts, D_MODEL)
    st_p = st_s = None
    mix = _mixer_weights(norm_gains, w_in, gla_w_lr, gla_b_lr, gla_norm, hgrn_lb_logits, hgrn_norm,
                         mlstm_i_bias, mlstm_f_bias, mlstm_norm, w_out)
    gains = mix["gains"]
    ffn1 = tuple(w.astype(BF16) for w in (ffn1_w_gate, ffn1_w_up, ffn1_w_down))
    ffn2 = tuple(w.astype(BF16) for w in (ffn2_w_gate, ffn2_w_up, ffn2_w_down))
    init = (state_gla, state_hgrn, state_mlstm_C, state_mlstm_n, state_mlstm_m)
    for l in range(DEPTH):
        yp, ys = _ffn(l, 0, yp, ys, gains, *ffn1)
        yp3, st_p = _mixer(l, yp.reshape(bp, tp, D_MODEL), mix, consts, None, st_p, nb=bp, r=PROMPT_CHUNK)
        ys, st_s = _mixer(l, ys, mix, consts, init, st_s, nb=SAMPLE_GROUP, r=SAMPLE_ROWS)
        yp, ys = _ffn(l, 4, yp3.reshape(bp * tp, D_MODEL), ys, gains, *ffn2)

    return (yp.reshape(bp, tp, D_MODEL), ys.reshape(bs, ts, D_MODEL)) + tuple(st_p) + tuple(st_s)
```

```python
import functools

import jax
import jax.numpy as jnp
import numpy as np
from jax import lax
from jax.experimental import pallas as pl
from jax.experimental.pallas import tpu as pltpu

F32 = jnp.float32
BF16 = jnp.bfloat16

D_MODEL = 1024
D_FF = 2688
DEPTH = 2
H = 4
DK_A, DV_A, RANK_A, TAU_A = 32, 64, 16, 16.0
E_B, DV_B = 64, 64
DK_C, DV_C = 128, 128
EPS = 1e-6
NEG_BIG = -1e30
M_INIT = -1e30
SPLIT_SIZES = (
    H * DK_A, H * DK_A, H * DV_A, H * DV_A, RANK_A,
    H * E_B, H * E_B, H * DV_B, H * DV_B,
    H * DK_C, H * DK_C, H * DV_C, H * DV_C, H, H,
)

LANES = 128
SUBLANES = 8
MXU_TILE = 256
VMEM_BYTES = 64 * 2**20
VMEM_TEMPORARIES = 16 * 2**20
GAIN_FFN1, GAIN_MIXER, GAIN_FFN2 = 0, 2, 4

SUB = SUBLANES
LA, VA = H * DK_A, H * DV_A
LB, VB = H * E_B, H * DV_B
N_FF_TILES = D_FF // MXU_TILE
TOKEN_TILE = 1024

C_AQ, C_AK, C_AV, C_AG = 0, LA, 2 * LA, 2 * LA + VA
C_BQ = C_AG + VA
C_BF, C_BI, C_BG = C_BQ + LB, C_BQ + 2 * LB, C_BQ + 3 * LB
C_CQ = C_BG + VB
C_CK, C_CV, C_CO = C_CQ + H * DK_C, C_CQ + 2 * H * DK_C, C_CQ + 2 * H * DK_C + H * DV_C
C_SMALL = C_CO + H * DV_C
C_FGATE = C_SMALL + LANES
N_IN_TILES = pl.cdiv(C_FGATE + LANES, MXU_TILE)
D_IN_PAD = N_IN_TILES * MXU_TILE
LANE_G = RANK_A

CHUNK_A, CHUNK_B = 64, 64
PAIR_GROUP = 8
FAST_DECAY_LIMIT = 60.0


def _rms(x, g):
    return x * lax.rsqrt(jnp.mean(x * x, axis=-1, keepdims=True) + EPS) * g


def _sigmoid(x):
    return 1.0 / (1.0 + jnp.exp(-x))


def _silu(x):
    return x * _sigmoid(x)


def _log_sigmoid(x):
    return jnp.minimum(x, 0.0) - jnp.log1p(jnp.exp(-jnp.abs(x)))


def _dot(a, b):
    return jnp.dot(a, b, preferred_element_type=F32)


def _dot_tn(a, b):
    return lax.dot_general(a, b, (((0,), (0,)), ((), ())), preferred_element_type=F32)


def _dot_nt(a, b):
    return lax.dot_general(a, b, (((1,), (1,)), ((), ())), preferred_element_type=F32)


def _tri_cumsum(tri16, x):
    t = tri16.shape[0]
    hi = x.astype(BF16)
    r1 = x - hi.astype(F32)
    mid = r1.astype(BF16)
    lo = (r1 - mid.astype(F32)).astype(BF16)
    out = []
    for s in range(0, x.shape[0], t):
        out.append(_dot(tri16, hi[s:s + t]) + _dot(tri16, mid[s:s + t]) + _dot(tri16, lo[s:s + t]))
    return out[0] if len(out) == 1 else jnp.concatenate(out, axis=0)


def _pad_rows(x, mult):
    pad = -x.shape[0] % mult
    return x if pad == 0 else jnp.concatenate([x, jnp.zeros((pad, x.shape[1]), x.dtype)], axis=0)


def _ffn_kernel(gain_row, n_main, x_ref, xs_ref, gains_ref, wg_ref, wu_ref, wd_ref, o_ref, os_ref, h_scr, acc_scr):
    gpre_ref, gpost_ref = gains_ref.at[gain_row:gain_row + 1, :], gains_ref.at[gain_row + 1:gain_row + 2, :]

    def token_tile(xin_ref, out_ref):
        rows = xin_ref.shape[0]
        h_ref, acc_ref = h_scr.at[0:rows, :], acc_scr.at[0:rows, :]
        h_ref[...] = _rms(xin_ref[...], gpre_ref[...]).astype(BF16)

        def ff_cols(cols):
            h = h_ref[...]
            g = _dot(h, wg_ref[:, cols])
            u = _dot(h, wu_ref[:, cols])
            return _dot((_silu(g) * u).astype(BF16), wd_ref[cols, :])

        acc_ref[...] = ff_cols(slice(N_FF_TILES * MXU_TILE, D_FF))
        for t in range(N_FF_TILES):
            acc_ref[...] += ff_cols(slice(t * MXU_TILE, (t + 1) * MXU_TILE))
        out_ref[...] = xin_ref[...] + _rms(acc_ref[...], 0.5 * gpost_ref[...])

    step = pl.program_id(0)
    pl.when(step < n_main)(lambda: token_tile(x_ref, o_ref))
    pl.when(step == n_main)(lambda: token_tile(xs_ref, os_ref))


def _const_spec(shape):
    return pl.BlockSpec(shape, lambda *_: (0,) * len(shape), pipeline_mode=pl.Buffered(1))


def _layer_spec(layer, shape):
    return pl.BlockSpec((None,) + tuple(shape), lambda *_: (layer,) + (0,) * len(shape),
                        pipeline_mode=pl.Buffered(1))


def _ffn(layer, gain_row, x, xs, gains, wg, wu, wd):
    n, ns = x.shape[0], xs.shape[0]
    tm = TOKEN_TILE
    n_main = n // tm
    assert n_main * tm == n and ns <= tm
    main_map = lambda i: (jnp.minimum(i, n_main - 1), 0)
    weights = 3 * D_MODEL * D_FF * 2
    tiles = (tm + ns) * D_MODEL * 2 * 2 * 4 + tm * D_MODEL * (2 + 4) + 4 * tm * MXU_TILE * 4
    return pl.pallas_call(
        functools.partial(_ffn_kernel, gain_row, n_main),
        name="swiglu_half_step",
        out_shape=[jax.ShapeDtypeStruct((n, D_MODEL), F32), jax.ShapeDtypeStruct((ns, D_MODEL), F32)],
        grid=(n_main + 1,),
        in_specs=[
            pl.BlockSpec((tm, D_MODEL), main_map),
            pl.BlockSpec((ns, D_MODEL), lambda i: (0, 0)),
            _layer_spec(layer, gains.shape[1:]),
            _layer_spec(layer, (D_MODEL, D_FF)),
            _layer_spec(layer, (D_MODEL, D_FF)),
            _layer_spec(layer, (D_FF, D_MODEL)),
        ],
        out_specs=[pl.BlockSpec((tm, D_MODEL), main_map), pl.BlockSpec((ns, D_MODEL), lambda i: (0, 0))],
        scratch_shapes=[pltpu.VMEM((tm, D_MODEL), BF16), pltpu.VMEM((tm, D_MODEL), F32)],
        compiler_params=pltpu.CompilerParams(
            dimension_semantics=("arbitrary",),
            vmem_limit_bytes=min(VMEM_BYTES, weights + tiles + VMEM_TEMPORARIES),
        ),
    )(x, xs, gains, wg, wu, wd)


def _gated_linear_exact(q, k, v, la, s_prev, pat32, pat16, tri_sub16):
    r, l = q.shape
    vv = v.shape[1]
    n_sub = r // SUB
    b = _tri_cumsum(tri_sub16, la)
    b3, q3, k3, v3 = (a.reshape(n_sub, SUB, a.shape[1]) for a in (b, q, k, v))
    bend3 = b3[:, SUB - 1:SUB, :]
    qh3 = q3 * jnp.exp(b3)
    kh3 = k3 * jnp.exp(bend3 - b3)

    tpos = lax.broadcasted_iota(jnp.int32, (n_sub, SUB, l), 1)
    intra3 = jnp.zeros((n_sub, SUB, vv), F32)
    for j in range(SUB):
        arg = jnp.where(tpos >= j, b3 - b3[:, j:j + 1, :], NEG_BIG)
        p = (q3 * k3[:, j:j + 1, :] * jnp.exp(arg)).reshape(r, l)
        sc = _dot(p.astype(BF16), pat16)
        intra3 = intra3 + sc.reshape(n_sub, SUB, vv) * v3[:, j:j + 1, :]

    dcols = jnp.exp(_pad_rows(bend3.reshape(n_sub, l), SUBLANES)).T
    s = s_prev
    inter = []
    for i in range(n_sub):
        inter.append(_dot(qh3[i].astype(BF16), s.astype(BF16)))
        u = _dot_tn(kh3[i].astype(BF16), v3[i].astype(BF16))
        s = s * dcols[:, i:i + 1] + u * pat32
    o = inter[0] if n_sub == 1 else jnp.concatenate(inter, axis=0)
    return o + intra3.reshape(r, vv), s


def _gated_linear_fast(q, k, v, bc, s_scr, o_store, pat32, nb, r, c):
    l, vv = q.shape[1], v.shape[1]
    n_ch = r // c
    n_tot = nb * n_ch
    bc3, q3, k3, v3 = (a.reshape(n_tot, c, a.shape[1]) for a in (bc, q, k, v))
    bmid = bc3[:, c // 2 - 1:c // 2, :]
    blast = bc3[:, c - 1:c, :]
    qt = (q3 * jnp.exp(bc3 - bmid)).astype(BF16)
    kt = (k3 * jnp.exp(bmid - bc3)).astype(BF16)
    qc = (q3 * jnp.exp(bc3)).astype(BF16)
    ke = (k3 * jnp.exp(blast - bc3)).astype(BF16)
    v16 = v3.astype(BF16)
    dcols = jnp.exp(_pad_rows(blast.reshape(n_tot, l), SUBLANES)).T

    head_l = lax.broadcasted_iota(jnp.int32, (c, l), 1) // (l // H)
    head_v = lax.broadcasted_iota(jnp.int32, (c, vv), 1) // (vv // H)
    causal = (lax.broadcasted_iota(jnp.int32, (H * c, c), 0) % c >= lax.broadcasted_iota(jnp.int32, (H * c, c), 1))
    zero16 = jnp.zeros((), BF16)

    idxs = range(n_tot)
    lhs = [jnp.concatenate([jnp.where(head_l == h, qt[i], zero16) for h in range(H)], axis=0) for i in idxs]
    sc = [_dot_nt(lhs[i], kt[i]) for i in idxs]
    sc = [jnp.where(causal, sc[i], 0.0).astype(BF16) for i in idxs]
    ov = [_dot(sc[i], v16[i]) for i in idxs]
    intra = [functools.reduce(jnp.add, [jnp.where(head_v == h, ov[i][h * c:(h + 1) * c], 0.0) for h in range(H)])
             for i in idxs]
    u = [_dot_tn(ke[i], v16[i]) for i in idxs]

    for ci in range(n_ch):
        ids = [b * n_ch + ci for b in range(nb)]
        s_old = [s_scr[b] for b in range(nb)]
        inter = [_dot(qc[i], s_old[b].astype(BF16)) for b, i in enumerate(ids)]
        for b, i in enumerate(ids):
            o_store(b, ci * c, c, intra[i] + inter[b])
            s_scr[b] = s_old[b] * dcols[:, i:i + 1] + u[i] * pat32


def _expand_heads(per_head):
    dk, dv = per_head[0].shape
    rows = []
    for h in range(H):
        parts = [jnp.zeros((dk, h * dv), F32), per_head[h], jnp.zeros((dk, (H - 1 - h) * dv), F32)]
        rows.append(jnp.concatenate([p for p in parts if p.shape[1]], axis=1))
    return jnp.concatenate(rows, axis=0)


def _head_block(s, h):
    dk, dv = s.shape[0] // H, s.shape[1] // H
    return s[h * dk:(h + 1) * dk, h * dv:(h + 1) * dv]


def _mixer_kernel(layer, nb, r, valid, has_init, n_steps, *refs):
    (x_ref, gains_ref, win_ref, wlr_ref, blr_ref, gnorm_ref, lb_ref, hnorm_ref,
     ibias_ref, fbias_ref, mnorm_ref, wout_ref, pa32_ref, pa16_ref, pb32_ref, pb16_ref, trisub_ref,
     tria_ref, trib_ref, tric_ref) = refs[:20]
    refs = refs[20:]
    gpre_ref = gains_ref.at[GAIN_MIXER:GAIN_MIXER + 1, :]
    gpost_ref = gains_ref.at[GAIN_MIXER + 1:GAIN_MIXER + 2, :]
    if has_init:
        gla0_ref, hgrn0_ref, c0_ref, n0_ref, m0_ref = refs[:5]
        refs = refs[5:]
    if layer:
        earlier_refs = refs[:5]
        refs = refs[5:]
    y_ref = refs[0]
    out_refs = refs[1:6]
    gla_ref, hgrn_ref, c_ref, n_ref, m_ref = (o.at[layer] for o in out_refs)
    hin_scr, proj_scr, mixed_scr, sgla_scr, shgrn_scr, cst_scr, nrow_scr, mrep_scr = refs[6:]
    n_as_column = not has_init
    rows_all = nb * r
    ca, cb = min(CHUNK_A, r), min(CHUNK_B, r)
    step = pl.program_id(0)

    if has_init:
        for b in range(nb):
            sgla_scr[b] = _expand_heads([gla0_ref[b, h] for h in range(H)])
            shgrn_scr[b] = _expand_heads([hgrn0_ref[b, h] for h in range(H)])
            for h in range(H):
                mrep_scr[b, h:h + 1, :] = jnp.broadcast_to(m0_ref[b:b + 1, h:h + 1], (1, LANES))
        cst_scr[...] = c0_ref[...]
        nrow_scr[...] = n0_ref[...]
    else:
        @pl.when(step == 0)
        def _():
            sgla_scr[...] = jnp.zeros_like(sgla_scr)
            shgrn_scr[...] = jnp.zeros_like(shgrn_scr)
            cst_scr[...] = jnp.zeros_like(cst_scr)
            mrep_scr[...] = jnp.full(mrep_scr.shape, M_INIT, F32)

    def load_x():
        if valid == r:
            return x_ref[...].reshape(rows_all, D_MODEL)
        x_in = x_ref[...]
        pad = jnp.zeros((r - valid, D_MODEL), F32)
        return jnp.concatenate([p for b in range(nb) for p in (x_in[b * valid:(b + 1) * valid], pad)], axis=0)

    hin_scr[...] = _rms(load_x(), gpre_ref[...]).astype(BF16)

    for t in range(N_IN_TILES):
        cols = slice(t * MXU_TILE, (t + 1) * MXU_TILE)
        proj_scr[:, cols] = _dot(hin_scr[...], win_ref[:, cols])

    lg = [lb_ref[j:j + 1, :] for j in range(DEPTH)]
    lg_max = functools.reduce(jnp.maximum, lg)
    ex = [jnp.exp(v - lg_max) for v in lg]
    ex_sum = functools.reduce(jnp.add, ex)
    lb = functools.reduce(jnp.add, [e / ex_sum for e in ex[1:layer + 1]], jnp.zeros_like(lg_max))

    if valid < r:
        row_ok_all = lax.broadcasted_iota(jnp.int32, (rows_all, 1), 0) % r < valid
        row_ok = row_ok_all[0:r]

    def store_cols(c0):
        def store(b, t0, n, val):
            mixed_scr[b * r + t0:b * r + t0 + n, c0:c0 + val.shape[1]] = val
        return store

    def exact_path(q_of, k_of, v_of, la_of, s_scr, pat32_ref, pat16_ref, c0):
        def one(bi, carry):
            rows = pl.ds(pl.multiple_of(bi * r, r), r)
            o, s1 = _gated_linear_exact(q_of(rows), k_of(rows), v_of(rows), la_of(rows), s_scr[bi],
                                        pat32_ref[...], pat16_ref[...], trisub_ref[...])
            mixed_scr[rows, c0:c0 + o.shape[1]] = o
            s_scr[bi] = s1
            return carry
        lax.fori_loop(0, nb, one, 0)

    def gated_linear(q_of, k_of, v_of, la_of, tri_ref, s_scr, pat32_ref, pat16_ref, c0, c):
        everything = slice(None)
        k_all = k_of(everything)
        bc = _tri_cumsum(tri_ref[...], la_of(everything, k_all))
        bc3 = bc.reshape(rows_all // c, c, bc.shape[1])
        to_mid, past_mid = -bc3[:, c // 2 - 1, :], bc3[:, c // 2 - 1, :] - bc3[:, c - 1, :]
        safe = jnp.max(jnp.maximum(to_mid, past_mid)) <= FAST_DECAY_LIMIT

        @pl.when(safe)
        def _():
            _gated_linear_fast(q_of(everything), k_all, v_of(everything), bc, s_scr,
                               store_cols(c0), pat32_ref[...], nb, r, c)

        @pl.when(jnp.logical_not(safe))
        def _():
            exact_path(q_of, k_of, v_of, la_of, s_scr, pat32_ref, pat16_ref, c0)

    def la_a(rows, k=None):
        small = proj_scr[rows, C_SMALL:C_SMALL + LANES]
        la = _log_sigmoid(_dot(small.astype(BF16), wlr_ref[...]) + blr_ref[...]) * (1.0 / TAU_A)
        if valid < r:
            la = jnp.where(row_ok_all if rows == slice(None) else row_ok, la, 0.0)
        return la

    gated_linear(lambda rows: proj_scr[rows, C_AQ:C_AQ + LA] * DK_A ** -0.5,
                 lambda rows: proj_scr[rows, C_AK:C_AK + LA],
                 lambda rows: proj_scr[rows, C_AV:C_AV + VA],
                 la_a, tria_ref, sgla_scr, pa32_ref, pa16_ref, 0, ca)

    def k_b(rows):
        kb = (1.0 - lb) * _sigmoid(-proj_scr[rows, C_BF:C_BF + LB])
        if valid < r:
            kb = jnp.where(row_ok_all if rows == slice(None) else row_ok, kb, 0.0)
        return kb

    gated_linear(lambda rows: _silu(proj_scr[rows, C_BQ:C_BQ + LB]), k_b,
                 lambda rows: proj_scr[rows, C_BI:C_BI + VB],
                 lambda rows, k=None: jnp.log1p(-(k_b(rows) if k is None else k)),
                 trib_ref, shgrn_scr, pb32_ref, pb16_ref, VA, cb)

    i_pre = proj_scr[:, C_SMALL:C_SMALL + LANES] + ibias_ref[...]
    lsf = _log_sigmoid(proj_scr[:, C_FGATE:C_FGATE + LANES] + fbias_ref[...])
    if valid < r:
        lsf = jnp.where(row_ok_all, lsf, 0.0)
        i_pre = jnp.where(row_ok_all, i_pre, NEG_BIG)
    f_cum = _tri_cumsum(tric_ref[...], lsf)
    a_all = i_pre - f_cum
    causal = (lax.broadcasted_iota(jnp.int32, (r, r), 0) >= lax.broadcasted_iota(jnp.int32, (r, r), 1))
    ones16 = jnp.ones((r, DV_C), BF16)

    def lanes(col):
        return jnp.broadcast_to(col, (col.shape[0], LANES))

    def head_cols(c0, b, h):
        return proj_scr[b * r:(b + 1) * r, c0 + h * DK_C:c0 + (h + 1) * DK_C]

    for b0 in range(0, nb, PAIR_GROUP // H):
        pairs = [(b, h) for b in range(b0, b0 + PAIR_GROUP // H) for h in range(H)]
        a_rows = {b: a_all[b * r:(b + 1) * r].T for b in range(b0, b0 + PAIR_GROUP // H)}
        f_rep = {(b, h): lanes(f_cum[b * r:(b + 1) * r, LANE_G + h:LANE_G + h + 1]) for b, h in pairs}
        a_rep = {(b, h): lanes(a_all[b * r:(b + 1) * r, LANE_G + h:LANE_G + h + 1]) for b, h in pairs}
        m_prev = {(b, h): mrep_scr[b, h:h + 1, :] for b, h in pairs}
        a_mat = {(b, h): jnp.where(causal, a_rows[b][LANE_G + h:LANE_G + h + 1, :], NEG_BIG) for b, h in pairs}
        g = {p: jnp.maximum(lanes(jnp.max(a_mat[p], axis=1, keepdims=True)), m_prev[p]) for p in pairs}
        d = {p: jnp.exp(a_mat[p] - g[p][:, 0:r]) for p in pairs}
        w_prev = {p: jnp.exp(m_prev[p] - g[p]) for p in pairs}
        m_t = {p: f_rep[p] + g[p] for p in pairs}
        q = {(b, h): head_cols(C_CQ, b, h) for b, h in pairs}
        q16 = {p: q[p].astype(BF16) for p in pairs}
        k = {(b, h): head_cols(C_CK, b, h) * DK_C ** -0.5 for b, h in pairs}
        v16 = {(b, h): head_cols(C_CV, b, h).astype(BF16) for b, h in pairs}
        if n_as_column:
            v16 = {p: jnp.concatenate([v16[p], ones16], axis=1) for p in pairs}
        qk = {p: _dot_nt(q16[p], k[p].astype(BF16)) * d[p] for p in pairs}
        c_prev = {(b, h): cst_scr[b, h] for b, h in pairs}
        now = {p: _dot(qk[p].astype(BF16), v16[p]) for p in pairs}
        old = {p: _dot(q16[p], c_prev[p].astype(BF16)) for p in pairs}
        for b, h in pairs:
            p = (b, h)
            if n_as_column:
                both = now[p] + jnp.concatenate([w_prev[p], w_prev[p]], axis=1) * old[p]
                num, den = both[:, 0:DV_C], both[:, DV_C:]
            else:
                num = now[p] + w_prev[p] * old[p]
                den = lanes(jnp.sum(qk[p], axis=1, keepdims=True)) + w_prev[p] * lanes(
                    jnp.sum(q[p] * nrow_scr[b, h:h + 1, :], axis=1, keepdims=True))
            mixed_scr[b * r:(b + 1) * r, VA + VB + h * DV_C:VA + VB + (h + 1) * DV_C] = (
                num / jnp.maximum(jnp.abs(den), jnp.exp(-m_t[p])))
        g_last = {p: g[p][r - 1:r, :] for p in pairs}
        w_old = {p: jnp.exp(m_prev[p] - g_last[p]) for p in pairs}
        kw = {p: k[p] * jnp.exp(a_rep[p] - g_last[p]) for p in pairs}
        upd = {p: _dot_tn(kw[p].astype(BF16), v16[p]) for p in pairs}
        for b, h in pairs:
            p = (b, h)
            if n_as_column:
                cst_scr[b, h] = jnp.concatenate([w_old[p], w_old[p]], axis=1) * c_prev[p] + upd[p]
            else:
                cst_scr[b, h] = w_old[p] * c_prev[p] + upd[p]
                nrow_scr[b, h:h + 1, :] = (w_old[p] * nrow_scr[b, h:h + 1, :]
                                           + jnp.sum(kw[p], axis=0, keepdims=True))
            mrep_scr[b, h:h + 1, :] = m_t[p][r - 1:r, :]

    def fold_states():
        for b in range(nb):
            for h in range(H):
                gla_ref[b, h] = _head_block(sgla_scr[b], h)
                hgrn_ref[b, h] = _head_block(shgrn_scr[b], h)
                c_ref[b, h] = cst_scr[b, h, :, 0:DV_C]
                if n_as_column:
                    n_ref[b, h:h + 1, :] = cst_scr[b, h, :, DV_C:].T[0:1, :]
                m_ref[b:b + 1, h:h + 1] = mrep_scr[b, h:h + 1, 0:1]
            if not n_as_column:
                n_ref[b] = nrow_scr[b]
        if layer:
            for prev, out in zip(earlier_refs, out_refs):
                out[0:layer] = prev[...]

    if has_init:
        fold_states()
    else:
        pl.when(step == n_steps - 1)(fold_states)

    o_a = mixed_scr[:, 0:VA]
    ms = _dot((o_a * o_a).astype(BF16), pb16_ref[...]) * (1.0 / DV_A)
    mixed_scr[:, 0:VA] = o_a * lax.rsqrt(ms + EPS) * gnorm_ref[...] * _silu(proj_scr[:, C_AG:C_AG + VA])
    o_b = mixed_scr[:, VA:VA + VB]
    ms = _dot((o_b * o_b).astype(BF16), pb16_ref[...]) * (1.0 / DV_B)
    mixed_scr[:, VA:VA + VB] = o_b * lax.rsqrt(ms + EPS) * hnorm_ref[...] * _silu(proj_scr[:, C_BG:C_BG + VB])
    for h in range(H):
        cols = slice(VA + VB + h * DV_C, VA + VB + (h + 1) * DV_C)
        gate = _sigmoid(proj_scr[:, C_CO + h * DV_C:C_CO + (h + 1) * DV_C])
        mixed_scr[:, cols] = gate * _rms(mixed_scr[:, cols], mnorm_ref[...])
    y = _dot(mixed_scr[...].astype(BF16), wout_ref[...])
    y = load_x() + _rms(y, gpost_ref[...])
    if valid == r:
        y_ref[...] = y.reshape(nb, r, D_MODEL)
    else:
        y_ref[...] = jnp.concatenate([y[b * r:b * r + valid] for b in range(nb)], axis=0)


def _block_tri(n, block):
    i, j = np.arange(n)[:, None], np.arange(n)[None, :]
    return jnp.asarray((j <= i) & (i // block == j // block), BF16)


def _mixer(layer, x, w, consts, init=None, earlier=None, *, nb, r):
    has_init = init is not None
    if has_init:
        batch = init[0].shape[1]
        valid = x.shape[0] // batch
        n_steps = batch // nb
        x_spec = pl.BlockSpec((nb * valid, D_MODEL), lambda i: (i, 0))
        b_idx = lambda i: i
    else:
        batch, tokens, _ = x.shape
        assert batch == nb
        valid = r
        n_steps = tokens // r
        x_spec = pl.BlockSpec((nb, r, D_MODEL), lambda i: (0, i, 0))
        b_idx = lambda i: 0
    rows_all = nb * r
    slab = min(MXU_TILE, rows_all)
    tris = [_block_tri(r, SUB), _block_tri(slab, min(CHUNK_A, r)), _block_tri(slab, min(CHUNK_B, r)),
            _block_tri(slab, r)]

    state_shapes = [(batch, H, DK_A, DV_A), (batch, H, E_B, DV_B), (batch, H, DK_C, DV_C), (batch, H, DK_C),
                    (batch, H)]

    def stacked_specs(n_layers):
        return [pl.BlockSpec((n_layers, nb) + s[1:],
                             (lambda nd: (lambda i: (0, b_idx(i)) + (0,) * (nd - 1)))(len(s))) for s in state_shapes]
    per_layer = [w["gains"], w["w_in"], w["w_lr"], w["b_lr"], w["gla_norm"]]
    per_layer2 = [w["hgrn_norm"], w["i_bias"], w["f_bias"], w["mlstm_norm"], w["w_out"]]
    shared = [consts["pa32"], consts["pa16"], consts["pb32"], consts["pb16"]] + tris
    in_arrays = [x] + per_layer + [w["lb_logits"]] + per_layer2 + shared
    in_specs = ([x_spec] + [_layer_spec(layer, a.shape[1:]) for a in per_layer]
                + [_const_spec(w["lb_logits"].shape)] + [_layer_spec(layer, a.shape[1:]) for a in per_layer2]
                + [_const_spec(a.shape) for a in shared])
    if has_init:
        in_arrays += list(init)
        in_specs += [pl.BlockSpec((None, nb) + s[1:], (lambda nd: (lambda i: (layer, i) + (0,) * (nd - 1)))(len(s)))
                     for s in state_shapes]
    if layer:
        in_arrays += list(earlier)
        in_specs += stacked_specs(layer)
    scratch = [
        pltpu.VMEM((rows_all, D_MODEL), BF16),
        pltpu.VMEM((rows_all, D_IN_PAD), F32),
        pltpu.VMEM((rows_all, D_MODEL), F32),
        pltpu.VMEM((nb, LA, VA), F32),
        pltpu.VMEM((nb, LB, VB), F32),
        pltpu.VMEM((nb, H, DK_C, DV_C if has_init else 2 * DV_C), F32),
        pltpu.VMEM((nb, H, DK_C), F32),
        pltpu.VMEM((nb, H, LANES), F32),
    ]
    state_block = nb * 4 * (LA * LANES + LB * LANES + H * DK_C * DV_C + SUBLANES * LANES + LANES)
    vmem = (2 * (D_MODEL * D_IN_PAD + D_MODEL * D_MODEL)
            + rows_all * (D_MODEL * (2 + 4 + 4 * 4) + D_IN_PAD * 4)
            + nb * 4 * (LA * VA + LB * VB + H * DK_C * 2 * DV_C + 2 * SUBLANES * LANES)
            + state_block * ((2 if has_init else 0) + 2 * layer + 2 * (layer + 1)))
    outs = pl.pallas_call(
        functools.partial(_mixer_kernel, layer, nb, r, valid, has_init, n_steps),
        name="token_mixer_sample" if has_init else "token_mixer_prompt",
        out_shape=([jax.ShapeDtypeStruct(x.shape, F32)]
                   + [jax.ShapeDtypeStruct((layer + 1,) + s, F32) for s in state_shapes]),
        grid=(n_steps,),
        in_specs=in_specs,
        out_specs=[x_spec] + stacked_specs(layer + 1),
        scratch_shapes=scratch,
        compiler_params=pltpu.CompilerParams(
            dimension_semantics=("arbitrary",),
            vmem_limit_bytes=min(VMEM_BYTES, vmem + VMEM_TEMPORARIES),
        ),
    )(*in_arrays)
    return outs[0], outs[1:]


def _head_pattern(rows_per_head, cols_per_head):
    rr = np.arange(H * rows_per_head)[:, None] // rows_per_head
    cc = np.arange(H * cols_per_head)[None, :] // cols_per_head
    return (rr == cc).astype(np.float32)


def _constants():
    pa, pb = _head_pattern(DK_A, DV_A), _head_pattern(E_B, DV_B)
    return {"pa32": jnp.asarray(pa), "pa16": jnp.asarray(pa, BF16),
            "pb32": jnp.asarray(pb), "pb16": jnp.asarray(pb, BF16)}


def _mixer_weights(norm_gains, w_in, gla_w_lr, gla_b_lr, gla_norm, hgrn_lb_logits, hgrn_norm,
                   mlstm_i_bias, mlstm_f_bias, mlstm_norm, w_out):
    (a_q, a_k, a_v, a_g, a_lr, b_q, b_f, b_i, b_g, c_q, c_k, c_v, c_o, c_i, c_f) = jnp.split(
        w_in, np.cumsum(SPLIT_SIZES)[:-1].tolist(), axis=2)
    zeros = lambda n: jnp.zeros((DEPTH, D_MODEL, n), F32)
    packed = jnp.concatenate(
        [a_q, a_k, a_v, a_g, b_q, b_f, b_i, b_g, c_q, c_k, c_v, c_o,
         a_lr, c_i, zeros(LANES - RANK_A - H), zeros(LANE_G), c_f, zeros(D_IN_PAD - C_FGATE - LANE_G - H)], axis=2)
    rows = lambda v: v.reshape(DEPTH, 1, -1).astype(F32)
    gate_lanes = lambda v: rows(jnp.pad(v, ((0, 0), (LANE_G, LANES - LANE_G - H))))
    return {
        "gains": norm_gains.astype(F32),
        "w_in": packed.astype(BF16),
        "w_lr": jnp.pad(gla_w_lr, ((0, 0), (0, LANES - RANK_A), (0, 0))).astype(BF16),
        "b_lr": rows(gla_b_lr),
        "gla_norm": rows(jnp.tile(gla_norm, (1, H))),
        "lb_logits": hgrn_lb_logits.astype(F32),
        "hgrn_norm": rows(jnp.tile(hgrn_norm, (1, H))),
        "i_bias": gate_lanes(mlstm_i_bias), "f_bias": gate_lanes(mlstm_f_bias),
        "mlstm_norm": rows(mlstm_norm),
        "w_out": w_out.astype(BF16),
    }


PROMPT_CHUNK = 64
SAMPLE_ROWS = SUBLANES
SAMPLE_GROUP = 8


def kernel(x_prompt, x_sample, state_gla, state_hgrn, state_mlstm_C, state_mlstm_n, state_mlstm_m, norm_gains, ffn1_w_gate, ffn1_w_up, ffn1_w_down, w_in, gla_w_lr, gla_b_lr, gla_norm, hgrn_lb_logits, hgrn_norm, mlstm_i_bias, mlstm_f_bias, mlstm_norm, w_out, ffn2_w_gate, ffn2_w_up, ffn2_w_down):
    bp, tp, _ = x_prompt.shape
    bs, ts, _ = x_sample.shape
    consts = _constants()
    yp = x_prompt.reshape(bp * tp, D_MODEL)
    ys = x_sample.reshape(bs * ts, D_MODEL)
    st_p = st_s = None
    mix = _mixer_weights(norm_gains, w_in, gla_w_lr, gla_b_lr, gla_norm, hgrn_lb_logits, hgrn_norm,
                         mlstm_i_bias, mlstm_f_bias, mlstm_norm, w_out)
    gains = mix["gains"]
    ffn1 = tuple(w.astype(BF16) for w in (ffn1_w_gate, ffn1_w_up, ffn1_w_down))
    ffn2 = tuple(w.astype(BF16) for w in (ffn2_w_gate, ffn2_w_up, ffn2_w_down))
    init = (state_gla, state_hgrn, state_mlstm_C, state_mlstm_n, state_mlstm_m)
    for l in range(DEPTH):
        yp, ys = _ffn(l, GAIN_FFN1, yp, ys, gains, *ffn1)
        yp3, st_p = _mixer(l, yp.reshape(bp, tp, D_MODEL), mix, consts, None, st_p, nb=bp, r=PROMPT_CHUNK)
        ys, st_s = _mixer(l, ys, mix, consts, init, st_s, nb=SAMPLE_GROUP, r=SAMPLE_ROWS)
        yp, ys = _ffn(l, GAIN_FFN2, yp3.reshape(bp * tp, D_MODEL), ys, gains, *ffn2)

    return (yp.reshape(bp, tp, D_MODEL), ys.reshape(bs, ts, D_MODEL)) + tuple(st_p) + tuple(st_s)
```

```python
import functools

import jax
import jax.numpy as jnp
import numpy as np
from jax import lax
from jax.experimental import pallas as pl
from jax.experimental.pallas import tpu as pltpu

F32 = jnp.float32
BF16 = jnp.bfloat16

D_MODEL = 1024
D_FF = 2688
DEPTH = 2
H = 4
DK_A, DV_A, RANK_A, TAU_A = 32, 64, 16, 16.0
E_B, DV_B = 64, 64
DK_C, DV_C = 128, 128
EPS = 1e-6
NEG_BIG = -1e30
M_INIT = -1e30
SPLIT_SIZES = (
    H * DK_A, H * DK_A, H * DV_A, H * DV_A, RANK_A,
    H * E_B, H * E_B, H * DV_B, H * DV_B,
    H * DK_C, H * DK_C, H * DV_C, H * DV_C, H, H,
)

LANES = 128
SUBLANES = 8
MXU_TILE = 256
VMEM_BYTES = 64 * 2**20
VMEM_TEMPORARIES = 16 * 2**20
GAIN_FFN1, GAIN_MIXER, GAIN_FFN2 = 0, 2, 4

SUB = SUBLANES
LA, VA = H * DK_A, H * DV_A
LB, VB = H * E_B, H * DV_B
N_FF_TILES = D_FF // MXU_TILE
TOKEN_TILE = 1024

C_AQ, C_AK, C_AV, C_AG = 0, LA, 2 * LA, 2 * LA + VA
C_BQ = C_AG + VA
C_BF, C_BI, C_BG = C_BQ + LB, C_BQ + 2 * LB, C_BQ + 3 * LB
C_CQ = C_BG + VB
C_CK, C_CV, C_CO = C_CQ + H * DK_C, C_CQ + 2 * H * DK_C, C_CQ + 2 * H * DK_C + H * DV_C
C_SMALL = C_CO + H * DV_C
C_FGATE = C_SMALL + LANES
N_IN_TILES = pl.cdiv(C_FGATE + LANES, MXU_TILE)
D_IN_PAD = N_IN_TILES * MXU_TILE
LANE_G = RANK_A

CHUNK_A, CHUNK_B = 64, 64
PAIR_GROUP = 8
FAST_DECAY_LIMIT = 60.0


def _rms(x, g):
    return x * lax.rsqrt(jnp.mean(x * x, axis=-1, keepdims=True) + EPS) * g


def _sigmoid(x):
    return 1.0 / (1.0 + jnp.exp(-x))


def _silu(x):
    return x * _sigmoid(x)


def _log_sigmoid(x):
    return jnp.minimum(x, 0.0) - jnp.log1p(jnp.exp(-jnp.abs(x)))


def _dot(a, b):
    return jnp.dot(a, b, preferred_element_type=F32)


def _dot_tn(a, b):
    return lax.dot_general(a, b, (((0,), (0,)), ((), ())), preferred_element_type=F32)


def _dot_nt(a, b):
    return lax.dot_general(a, b, (((1,), (1,)), ((), ())), preferred_element_type=F32)


def _tri_cumsum(tri16, x):
    t = tri16.shape[0]
    hi = x.astype(BF16)
    r1 = x - hi.astype(F32)
    mid = r1.astype(BF16)
    lo = (r1 - mid.astype(F32)).astype(BF16)
    out = []
    for s in range(0, x.shape[0], t):
        out.append(_dot(tri16, hi[s:s + t]) + _dot(tri16, mid[s:s + t]) + _dot(tri16, lo[s:s + t]))
    return out[0] if len(out) == 1 else jnp.concatenate(out, axis=0)


def _pad_rows(x, mult):
    pad = -x.shape[0] % mult
    return x if pad == 0 else jnp.concatenate([x, jnp.zeros((pad, x.shape[1]), x.dtype)], axis=0)


def _ffn_kernel(gain_row, n_main, x_ref, xs_ref, gains_ref, wg_ref, wu_ref, wd_ref, o_ref, os_ref, h_scr, acc_scr):
    gpre_ref, gpost_ref = gains_ref.at[gain_row:gain_row + 1, :], gains_ref.at[gain_row + 1:gain_row + 2, :]

    def token_tile(xin_ref, out_ref):
        rows = xin_ref.shape[0]
        h_ref, acc_ref = h_scr.at[0:rows, :], acc_scr.at[0:rows, :]
        h_ref[...] = _rms(xin_ref[...], gpre_ref[...]).astype(BF16)

        def ff_cols(cols):
            h = h_ref[...]
            g = _dot(h, wg_ref[:, cols])
            u = _dot(h, wu_ref[:, cols])
            return _dot((_silu(g) * u).astype(BF16), wd_ref[cols, :])

        acc_ref[...] = ff_cols(slice(N_FF_TILES * MXU_TILE, D_FF))
        for t in range(N_FF_TILES):
            acc_ref[...] += ff_cols(slice(t * MXU_TILE, (t + 1) * MXU_TILE))
        out_ref[...] = xin_ref[...] + _rms(acc_ref[...], 0.5 * gpost_ref[...])

    step = pl.program_id(0)
    pl.when(step < n_main)(lambda: token_tile(x_ref, o_ref))
    pl.when(step == n_main)(lambda: token_tile(xs_ref, os_ref))


def _const_spec(shape):
    return pl.BlockSpec(shape, lambda *_: (0,) * len(shape), pipeline_mode=pl.Buffered(1))


def _layer_spec(layer, shape):
    return pl.BlockSpec((None,) + tuple(shape), lambda *_: (layer,) + (0,) * len(shape),
                        pipeline_mode=pl.Buffered(1))


def _ffn(layer, gain_row, x, xs, gains, wg, wu, wd):
    n, ns = x.shape[0], xs.shape[0]
    tm = TOKEN_TILE
    n_main = n // tm
    assert n_main * tm == n and ns <= tm
    main_map = lambda i: (jnp.minimum(i, n_main - 1), 0)
    weights = 3 * D_MODEL * D_FF * 2
    tiles = (tm + ns) * D_MODEL * 2 * 2 * 4 + tm * D_MODEL * (2 + 4) + 4 * tm * MXU_TILE * 4
    return pl.pallas_call(
        functools.partial(_ffn_kernel, gain_row, n_main),
        name="swiglu_half_step",
        out_shape=[jax.ShapeDtypeStruct((n, D_MODEL), F32), jax.ShapeDtypeStruct((ns, D_MODEL), F32)],
        grid=(n_main + 1,),
        in_specs=[
            pl.BlockSpec((tm, D_MODEL), main_map),
            pl.BlockSpec((ns, D_MODEL), lambda i: (0, 0)),
            _layer_spec(layer, gains.shape[1:]),
            _layer_spec(layer, (D_MODEL, D_FF)),
            _layer_spec(layer, (D_MODEL, D_FF)),
            _layer_spec(layer, (D_FF, D_MODEL)),
        ],
        out_specs=[pl.BlockSpec((tm, D_MODEL), main_map), pl.BlockSpec((ns, D_MODEL), lambda i: (0, 0))],
        scratch_shapes=[pltpu.VMEM((tm, D_MODEL), BF16), pltpu.VMEM((tm, D_MODEL), F32)],
        compiler_params=pltpu.CompilerParams(
            dimension_semantics=("arbitrary",),
            vmem_limit_bytes=min(VMEM_BYTES, weights + tiles + VMEM_TEMPORARIES),
        ),
    )(x, xs, gains, wg, wu, wd)


def _gated_linear_exact(q, k, v, la, s_prev, pat32, pat16, tri_sub16):
    r, l = q.shape
    vv = v.shape[1]
    n_sub = r // SUB
    b = _tri_cumsum(tri_sub16, la)
    b3, q3, k3, v3 = (a.reshape(n_sub, SUB, a.shape[1]) for a in (b, q, k, v))
    bend3 = b3[:, SUB - 1:SUB, :]
    qh3 = q3 * jnp.exp(b3)
    kh3 = k3 * jnp.exp(bend3 - b3)

    tpos = lax.broadcasted_iota(jnp.int32, (n_sub, SUB, l), 1)
    intra3 = jnp.zeros((n_sub, SUB, vv), F32)
    for j in range(SUB):
        arg = jnp.where(tpos >= j, b3 - b3[:, j:j + 1, :], NEG_BIG)
        p = (q3 * k3[:, j:j + 1, :] * jnp.exp(arg)).reshape(r, l)
        sc = _dot(p.astype(BF16), pat16)
        intra3 = intra3 + sc.reshape(n_sub, SUB, vv) * v3[:, j:j + 1, :]

    dcols = jnp.exp(_pad_rows(bend3.reshape(n_sub, l), SUBLANES)).T
    s = s_prev
    inter = []
    for i in range(n_sub):
        inter.append(_dot(qh3[i].astype(BF16), s.astype(BF16)))
        u = _dot_tn(kh3[i].astype(BF16), v3[i].astype(BF16))
        s = s * dcols[:, i:i + 1] + u * pat32
    o = inter[0] if n_sub == 1 else jnp.concatenate(inter, axis=0)
    return o + intra3.reshape(r, vv), s


def _gated_linear_fast(q, k, v, bc, s_scr, o_store, pat32, nb, r, c):
    l, vv = q.shape[1], v.shape[1]
    n_ch = r // c
    n_tot = nb * n_ch
    bc3, q3, k3, v3 = (a.reshape(n_tot, c, a.shape[1]) for a in (bc, q, k, v))
    bmid = bc3[:, c // 2 - 1:c // 2, :]
    blast = bc3[:, c - 1:c, :]
    qt = (q3 * jnp.exp(bc3 - bmid)).astype(BF16)
    kt = (k3 * jnp.exp(bmid - bc3)).astype(BF16)
    qc = (q3 * jnp.exp(bc3)).astype(BF16)
    ke = (k3 * jnp.exp(blast - bc3)).astype(BF16)
    v16 = v3.astype(BF16)
    dcols = jnp.exp(_pad_rows(blast.reshape(n_tot, l), SUBLANES)).T

    head_l = lax.broadcasted_iota(jnp.int32, (c, l), 1) // (l // H)
    head_v = lax.broadcasted_iota(jnp.int32, (c, vv), 1) // (vv // H)
    causal = (lax.broadcasted_iota(jnp.int32, (H * c, c), 0) % c >= lax.broadcasted_iota(jnp.int32, (H * c, c), 1))
    zero16 = jnp.zeros((), BF16)

    idxs = range(n_tot)
    lhs = [jnp.concatenate([jnp.where(head_l == h, qt[i], zero16) for h in range(H)], axis=0) for i in idxs]
    sc = [_dot_nt(lhs[i], kt[i]) for i in idxs]
    sc = [jnp.where(causal, sc[i], 0.0).astype(BF16) for i in idxs]
    ov = [_dot(sc[i], v16[i]) for i in idxs]
    intra = [functools.reduce(jnp.add, [jnp.where(head_v == h, ov[i][h * c:(h + 1) * c], 0.0) for h in range(H)])
             for i in idxs]
    u = [_dot_tn(ke[i], v16[i]) for i in idxs]

    for ci in range(n_ch):
        ids = [b * n_ch + ci for b in range(nb)]
        s_old = [s_scr[b] for b in range(nb)]
        inter = [_dot(qc[i], s_old[b].astype(BF16)) for b, i in enumerate(ids)]
        for b, i in enumerate(ids):
            o_store(b, ci * c, c, intra[i] + inter[b])
            s_scr[b] = s_old[b] * dcols[:, i:i + 1] + u[i] * pat32


def _expand_heads(per_head):
    dk, dv = per_head[0].shape
    rows = []
    for h in range(H):
        parts = [jnp.zeros((dk, h * dv), F32), per_head[h], jnp.zeros((dk, (H - 1 - h) * dv), F32)]
        rows.append(jnp.concatenate([p for p in parts if p.shape[1]], axis=1))
    return jnp.concatenate(rows, axis=0)


def _head_block(s, h):
    dk, dv = s.shape[0] // H, s.shape[1] // H
    return s[h * dk:(h + 1) * dk, h * dv:(h + 1) * dv]


def _mixer_kernel(layer, nb, r, valid, has_init, n_steps, *refs):
    (x_ref, gains_ref, win_ref, wlr_ref, blr_ref, gnorm_ref, lb_ref, hnorm_ref,
     ibias_ref, fbias_ref, mnorm_ref, wout_ref, pa32_ref, pa16_ref, pb32_ref, pb16_ref, trisub_ref,
     tria_ref, trib_ref, tric_ref) = refs[:20]
    refs = refs[20:]
    gpre_ref = gains_ref.at[GAIN_MIXER:GAIN_MIXER + 1, :]
    gpost_ref = gains_ref.at[GAIN_MIXER + 1:GAIN_MIXER + 2, :]
    if has_init:
        gla0_ref, hgrn0_ref, c0_ref, n0_ref, m0_ref = refs[:5]
        refs = refs[5:]
    if layer:
        earlier_refs = refs[:5]
        refs = refs[5:]
    y_ref = refs[0]
    out_refs = refs[1:6]
    gla_ref, hgrn_ref, c_ref, n_ref, m_ref = (o.at[layer] for o in out_refs)
    hin_scr, proj_scr, mixed_scr, sgla_scr, shgrn_scr, cst_scr, nrow_scr, mrep_scr = refs[6:]
    n_as_column = not has_init
    rows_all = nb * r
    ca, cb = min(CHUNK_A, r), min(CHUNK_B, r)
    step = pl.program_id(0)

    if has_init:
        for b in range(nb):
            sgla_scr[b] = _expand_heads([gla0_ref[b, h] for h in range(H)])
            shgrn_scr[b] = _expand_heads([hgrn0_ref[b, h] for h in range(H)])
            for h in range(H):
                mrep_scr[b, h:h + 1, :] = jnp.broadcast_to(m0_ref[b:b + 1, h:h + 1], (1, LANES))
        cst_scr[...] = c0_ref[...]
        nrow_scr[...] = n0_ref[...]
    else:
        @pl.when(step == 0)
        def _():
            sgla_scr[...] = jnp.zeros_like(sgla_scr)
            shgrn_scr[...] = jnp.zeros_like(shgrn_scr)
            cst_scr[...] = jnp.zeros_like(cst_scr)
            mrep_scr[...] = jnp.full(mrep_scr.shape, M_INIT, F32)

    def load_x():
        if valid == r:
            return x_ref[...].reshape(rows_all, D_MODEL)
        x_in = x_ref[...]
        pad = jnp.zeros((r - valid, D_MODEL), F32)
        return jnp.concatenate([p for b in range(nb) for p in (x_in[b * valid:(b + 1) * valid], pad)], axis=0)

    hin_scr[...] = _rms(load_x(), gpre_ref[...]).astype(BF16)

    for t in range(N_IN_TILES):
        cols = slice(t * MXU_TILE, (t + 1) * MXU_TILE)
        proj_scr[:, cols] = _dot(hin_scr[...], win_ref[:, cols])

    lg = [lb_ref[j:j + 1, :] for j in range(DEPTH)]
    lg_max = functools.reduce(jnp.maximum, lg)
    ex = [jnp.exp(v - lg_max) for v in lg]
    ex_sum = functools.reduce(jnp.add, ex)
    lb = functools.reduce(jnp.add, [e / ex_sum for e in ex[1:layer + 1]], jnp.zeros_like(lg_max))

    if valid < r:
        row_ok_all = lax.broadcasted_iota(jnp.int32, (rows_all, 1), 0) % r < valid
        row_ok = row_ok_all[0:r]

    def store_cols(c0):
        def store(b, t0, n, val):
            mixed_scr[b * r + t0:b * r + t0 + n, c0:c0 + val.shape[1]] = val
        return store

    def exact_path(q_of, k_of, v_of, la_of, s_scr, pat32_ref, pat16_ref, c0):
        def one(bi, carry):
            rows = pl.ds(pl.multiple_of(bi * r, r), r)
            o, s1 = _gated_linear_exact(q_of(rows), k_of(rows), v_of(rows), la_of(rows), s_scr[bi],
                                        pat32_ref[...], pat16_ref[...], trisub_ref[...])
            mixed_scr[rows, c0:c0 + o.shape[1]] = o
            s_scr[bi] = s1
            return carry
        lax.fori_loop(0, nb, one, 0)

    def gated_linear(q_of, k_of, v_of, la_of, tri_ref, s_scr, pat32_ref, pat16_ref, c0, c):
        everything = slice(None)
        k_all = k_of(everything)
        bc = _tri_cumsum(tri_ref[...], la_of(everything, k_all))
        bc3 = bc.reshape(rows_all // c, c, bc.shape[1])
        to_mid, past_mid = -bc3[:, c // 2 - 1, :], bc3[:, c // 2 - 1, :] - bc3[:, c - 1, :]
        safe = jnp.max(jnp.maximum(to_mid, past_mid)) <= FAST_DECAY_LIMIT

        @pl.when(safe)
        def _():
            _gated_linear_fast(q_of(everything), k_all, v_of(everything), bc, s_scr,
                               store_cols(c0), pat32_ref[...], nb, r, c)

        @pl.when(jnp.logical_not(safe))
        def _():
            exact_path(q_of, k_of, v_of, la_of, s_scr, pat32_ref, pat16_ref, c0)

    def la_a(rows, k=None):
        small = proj_scr[rows, C_SMALL:C_SMALL + LANES]
        la = _log_sigmoid(_dot(small.astype(BF16), wlr_ref[...]) + blr_ref[...]) * (1.0 / TAU_A)
        if valid < r:
            la = jnp.where(row_ok_all if rows == slice(None) else row_ok, la, 0.0)
        return la

    gated_linear(lambda rows: proj_scr[rows, C_AQ:C_AQ + LA] * DK_A ** -0.5,
                 lambda rows: proj_scr[rows, C_AK:C_AK + LA],
                 lambda rows: proj_scr[rows, C_AV:C_AV + VA],
                 la_a, tria_ref, sgla_scr, pa32_ref, pa16_ref, 0, ca)

    def k_b(rows):
        kb = (1.0 - lb) * _sigmoid(-proj_scr[rows, C_BF:C_BF + LB])
        if valid < r:
            kb = jnp.where(row_ok_all if rows == slice(None) else row_ok, kb, 0.0)
        return kb

    gated_linear(lambda rows: _silu(proj_scr[rows, C_BQ:C_BQ + LB]), k_b,
                 lambda rows: proj_scr[rows, C_BI:C_BI + VB],
                 lambda rows, k=None: jnp.log1p(-(k_b(rows) if k is None else k)),
                 trib_ref, shgrn_scr, pb32_ref, pb16_ref, VA, cb)

    i_pre = proj_scr[:, C_SMALL:C_SMALL + LANES] + ibias_ref[...]
    lsf = _log_sigmoid(proj_scr[:, C_FGATE:C_FGATE + LANES] + fbias_ref[...])
    if valid < r:
        lsf = jnp.where(row_ok_all, lsf, 0.0)
        i_pre = jnp.where(row_ok_all, i_pre, NEG_BIG)
    f_cum = _tri_cumsum(tric_ref[...], lsf)
    a_all = i_pre - f_cum
    causal = (lax.broadcasted_iota(jnp.int32, (r, r), 0) >= lax.broadcasted_iota(jnp.int32, (r, r), 1))
    ones16 = jnp.ones((r, DV_C), BF16)

    def lanes(col):
        return jnp.broadcast_to(col, (col.shape[0], LANES))

    def head_cols(c0, b, h):
        return proj_scr[b * r:(b + 1) * r, c0 + h * DK_C:c0 + (h + 1) * DK_C]

    for b0 in range(0, nb, PAIR_GROUP // H):
        pairs = [(b, h) for b in range(b0, b0 + PAIR_GROUP // H) for h in range(H)]
        a_rows = {b: a_all[b * r:(b + 1) * r].T for b in range(b0, b0 + PAIR_GROUP // H)}
        f_rep = {(b, h): lanes(f_cum[b * r:(b + 1) * r, LANE_G + h:LANE_G + h + 1]) for b, h in pairs}
        a_rep = {(b, h): lanes(a_all[b * r:(b + 1) * r, LANE_G + h:LANE_G + h + 1]) for b, h in pairs}
        m_prev = {(b, h): mrep_scr[b, h:h + 1, :] for b, h in pairs}
        a_mat = {(b, h): jnp.where(causal, a_rows[b][LANE_G + h:LANE_G + h + 1, :], NEG_BIG) for b, h in pairs}
        g = {p: jnp.maximum(lanes(jnp.max(a_mat[p], axis=1, keepdims=True)), m_prev[p]) for p in pairs}
        d = {p: jnp.exp(a_mat[p] - g[p][:, 0:r]) for p in pairs}
        w_prev = {p: jnp.exp(m_prev[p] - g[p]) for p in pairs}
        m_t = {p: f_rep[p] + g[p] for p in pairs}
        q = {(b, h): head_cols(C_CQ, b, h) for b, h in pairs}
        q16 = {p: q[p].astype(BF16) for p in pairs}
        k = {(b, h): head_cols(C_CK, b, h) * DK_C ** -0.5 for b, h in pairs}
        v16 = {(b, h): head_cols(C_CV, b, h).astype(BF16) for b, h in pairs}
        if n_as_column:
            v16 = {p: jnp.concatenate([v16[p], ones16], axis=1) for p in pairs}
        qk = {p: _dot_nt(q16[p], k[p].astype(BF16)) * d[p] for p in pairs}
        c_prev = {(b, h): cst_scr[b, h] for b, h in pairs}
        now = {p: _dot(qk[p].astype(BF16), v16[p]) for p in pairs}
        old = {p: _dot(q16[p], c_prev[p].astype(BF16)) for p in pairs}
        for b, h in pairs:
            p = (b, h)
            if n_as_column:
                both = now[p] + jnp.concatenate([w_prev[p], w_prev[p]], axis=1) * old[p]
                num, den = both[:, 0:DV_C], both[:, DV_C:]
            else:
                num = now[p] + w_prev[p] * old[p]
                den = lanes(jnp.sum(qk[p], axis=1, keepdims=True)) + w_prev[p] * lanes(
                    jnp.sum(q[p] * nrow_scr[b, h:h + 1, :], axis=1, keepdims=True))
            mixed_scr[b * r:(b + 1) * r, VA + VB + h * DV_C:VA + VB + (h + 1) * DV_C] = (
                num / jnp.maximum(jnp.abs(den), jnp.exp(-m_t[p])))
        g_last = {p: g[p][r - 1:r, :] for p in pairs}
        w_old = {p: jnp.exp(m_prev[p] - g_last[p]) for p in pairs}
        kw = {p: k[p] * jnp.exp(a_rep[p] - g_last[p]) for p in pairs}
        upd = {p: _dot_tn(kw[p].astype(BF16), v16[p]) for p in pairs}
        for b, h in pairs:
            p = (b, h)
            if n_as_column:
                cst_scr[b, h] = jnp.concatenate([w_old[p], w_old[p]], axis=1) * c_prev[p] + upd[p]
            else:
                cst_scr[b, h] = w_old[p] * c_prev[p] + upd[p]
                nrow_scr[b, h:h + 1, :] = (w_old[p] * nrow_scr[b, h:h + 1, :]
                                           + jnp.sum(kw[p], axis=0, keepdims=True))
            mrep_scr[b, h:h + 1, :] = m_t[p][r - 1:r, :]

    def fold_states():
        for b in range(nb):
            for h in range(H):
                gla_ref[b, h] = _head_block(sgla_scr[b], h)
                hgrn_ref[b, h] = _head_block(shgrn_scr[b], h)
                c_ref[b, h] = cst_scr[b, h, :, 0:DV_C]
                if n_as_column:
                    n_ref[b, h:h + 1, :] = cst_scr[b, h, :, DV_C:].T[0:1, :]
                m_ref[b:b + 1, h:h + 1] = mrep_scr[b, h:h + 1, 0:1]
            if not n_as_column:
                n_ref[b] = nrow_scr[b]
        if layer:
            for prev, out in zip(earlier_refs, out_refs):
                out[0:layer] = prev[...]

    if has_init:
        fold_states()
    else:
        pl.when(step == n_steps - 1)(fold_states)

    o_a = mixed_scr[:, 0:VA]
    ms = _dot((o_a * o_a).astype(BF16), pb16_ref[...]) * (1.0 / DV_A)
    mixed_scr[:, 0:VA] = o_a * lax.rsqrt(ms + EPS) * gnorm_ref[...] * _silu(proj_scr[:, C_AG:C_AG + VA])
    o_b = mixed_scr[:, VA:VA + VB]
    ms = _dot((o_b * o_b).astype(BF16), pb16_ref[...]) * (1.0 / DV_B)
    mixed_scr[:, VA:VA + VB] = o_b * lax.rsqrt(ms + EPS) * hnorm_ref[...] * _silu(proj_scr[:, C_BG:C_BG + VB])
    for h in range(H):
        cols = slice(VA + VB + h * DV_C, VA + VB + (h + 1) * DV_C)
        gate = _sigmoid(proj_scr[:, C_CO + h * DV_C:C_CO + (h + 1) * DV_C])
        mixed_scr[:, cols] = gate * _rms(mixed_scr[:, cols], mnorm_ref[...])
    y = _dot(mixed_scr[...].astype(BF16), wout_ref[...])
    y = load_x() + _rms(y, gpost_ref[...])
    if valid == r:
        y_ref[...] = y.reshape(nb, r, D_MODEL)
    else:
        y_ref[...] = jnp.concatenate([y[b * r:b * r + valid] for b in range(nb)], axis=0)


def _block_tri(n, block):
    i, j = np.arange(n)[:, None], np.arange(n)[None, :]
    return jnp.asarray((j <= i) & (i // block == j // block), BF16)


def _mixer(layer, x, w, consts, init=None, earlier=None, *, nb, r):
    has_init = init is not None
    if has_init:
        batch = init[0].shape[1]
        valid = x.shape[0] // batch
        n_steps = batch // nb
        x_spec = pl.BlockSpec((nb * valid, D_MODEL), lambda i: (i, 0))
        b_idx = lambda i: i
    else:
        batch, tokens, _ = x.shape
        assert batch == nb
        valid = r
        n_steps = tokens // r
        x_spec = pl.BlockSpec((nb, r, D_MODEL), lambda i: (0, i, 0))
        b_idx = lambda i: 0
    rows_all = nb * r
    slab = min(MXU_TILE, rows_all)
    tris = [_block_tri(r, SUB), _block_tri(slab, min(CHUNK_A, r)), _block_tri(slab, min(CHUNK_B, r)),
            _block_tri(slab, r)]

    state_shapes = [(batch, H, DK_A, DV_A), (batch, H, E_B, DV_B), (batch, H, DK_C, DV_C), (batch, H, DK_C),
                    (batch, H)]

    def stacked_specs(n_layers):
        return [pl.BlockSpec((n_layers, nb) + s[1:],
                             (lambda nd: (lambda i: (0, b_idx(i)) + (0,) * (nd - 1)))(len(s))) for s in state_shapes]
    per_layer = [w["gains"], w["w_in"], w["w_lr"], w["b_lr"], w["gla_norm"]]
    per_layer2 = [w["hgrn_norm"], w["i_bias"], w["f_bias"], w["mlstm_norm"], w["w_out"]]
    shared = [consts["pa32"], consts["pa16"], consts["pb32"], consts["pb16"]] + tris
    in_arrays = [x] + per_layer + [w["lb_logits"]] + per_layer2 + shared
    in_specs = ([x_spec] + [_layer_spec(layer, a.shape[1:]) for a in per_layer]
                + [_const_spec(w["lb_logits"].shape)] + [_layer_spec(layer, a.shape[1:]) for a in per_layer2]
                + [_const_spec(a.shape) for a in shared])
    if has_init:
        in_arrays += list(init)
        in_specs += [pl.BlockSpec((None, nb) + s[1:], (lambda nd: (lambda i: (layer, i) + (0,) * (nd - 1)))(len(s)))
                     for s in state_shapes]
    if layer:
        in_arrays += list(earlier)
        in_specs += stacked_specs(layer)
    scratch = [
        pltpu.VMEM((rows_all, D_MODEL), BF16),
        pltpu.VMEM((rows_all, D_IN_PAD), F32),
        pltpu.VMEM((rows_all, D_MODEL), F32),
        pltpu.VMEM((nb, LA, VA), F32),
        pltpu.VMEM((nb, LB, VB), F32),
        pltpu.VMEM((nb, H, DK_C, DV_C if has_init else 2 * DV_C), F32),
        pltpu.VMEM((nb, H, DK_C), F32),
        pltpu.VMEM((nb, H, LANES), F32),
    ]
    state_block = nb * 4 * (LA * LANES + LB * LANES + H * DK_C * DV_C + SUBLANES * LANES + LANES)
    vmem = (2 * (D_MODEL * D_IN_PAD + D_MODEL * D_MODEL)
            + rows_all * (D_MODEL * (2 + 4 + 4 * 4) + D_IN_PAD * 4)
            + nb * 4 * (LA * VA + LB * VB + H * DK_C * 2 * DV_C + 2 * SUBLANES * LANES)
            + state_block * ((2 if has_init else 0) + 2 * layer + 2 * (layer + 1)))
    outs = pl.pallas_call(
        functools.partial(_mixer_kernel, layer, nb, r, valid, has_init, n_steps),
        name="token_mixer_sample" if has_init else "token_mixer_prompt",
        out_shape=([jax.ShapeDtypeStruct(x.shape, F32)]
                   + [jax.ShapeDtypeStruct((layer + 1,) + s, F32) for s in state_shapes]),
        grid=(n_steps,),
        in_specs=in_specs,
        out_specs=[x_spec] + stacked_specs(layer + 1),
        scratch_shapes=scratch,
        compiler_params=pltpu.CompilerParams(
            dimension_semantics=("arbitrary",),
            vmem_limit_bytes=min(VMEM_BYTES, vmem + VMEM_TEMPORARIES),
        ),
    )(*in_arrays)
    return outs[0], outs[1:]


def _head_pattern(rows_per_head, cols_per_head):
    rr = np.arange(H * rows_per_head)[:, None] // rows_per_head
    cc = np.arange(H * cols_per_head)[None, :] // cols_per_head
    return (rr == cc).astype(np.float32)


def _constants():
    pa, pb = _head_pattern(DK_A, DV_A), _head_pattern(E_B, DV_B)
    return {"pa32": jnp.asarray(pa), "pa16": jnp.asarray(pa, BF16),
            "pb32": jnp.asarray(pb), "pb16": jnp.asarray(pb, BF16)}


def _mixer_weights(norm_gains, w_in, gla_w_lr, gla_b_lr, gla_norm, hgrn_lb_logits, hgrn_norm,
                   mlstm_i_bias, mlstm_f_bias, mlstm_norm, w_out):
    (a_q, a_k, a_v, a_g, a_lr, b_q, b_f, b_i, b_g, c_q, c_k, c_v, c_o, c_i, c_f) = jnp.split(
        w_in, np.cumsum(SPLIT_SIZES)[:-1].tolist(), axis=2)
    zeros = lambda n: jnp.zeros((DEPTH, D_MODEL, n), F32)
    packed = jnp.concatenate(
        [a_q, a_k, a_v, a_g, b_q, b_f, b_i, b_g, c_q, c_k, c_v, c_o,
         a_lr, c_i, zeros(LANES - RANK_A - H), zeros(LANE_G), c_f, zeros(D_IN_PAD - C_FGATE - LANE_G - H)], axis=2)
    rows = lambda v: v.reshape(DEPTH, 1, -1).astype(F32)
    gate_lanes = lambda v: rows(jnp.pad(v, ((0, 0), (LANE_G, LANES - LANE_G - H))))
    return {
        "gains": norm_gains.astype(F32),
        "w_in": packed.astype(BF16),
        "w_lr": jnp.pad(gla_w_lr, ((0, 0), (0, LANES - RANK_A), (0, 0))).astype(BF16),
        "b_lr": rows(gla_b_lr),
        "gla_norm": rows(jnp.tile(gla_norm, (1, H))),
        "lb_logits": hgrn_lb_logits.astype(F32),
        "hgrn_norm": rows(jnp.tile(hgrn_norm, (1, H))),
        "i_bias": gate_lanes(mlstm_i_bias), "f_bias": gate_lanes(mlstm_f_bias),
        "mlstm_norm": rows(mlstm_norm),
        "w_out": w_out.astype(BF16),
    }


PROMPT_CHUNK = 64
SAMPLE_ROWS = SUBLANES
STATE_WINDOW_BUDGET = 32 * 2**20


def _sample_group(layer):
    per_sequence = 4 * (LA * LANES + LB * LANES + H * DK_C * DV_C + SUBLANES * LANES + LANES)
    windows = 2 * (1 + layer + layer + 1)
    nb = SUBLANES
    while 2 * nb * per_sequence * windows <= STATE_WINDOW_BUDGET:
        nb *= 2
    return nb


def kernel(x_prompt, x_sample, state_gla, state_hgrn, state_mlstm_C, state_mlstm_n, state_mlstm_m, norm_gains, ffn1_w_gate, ffn1_w_up, ffn1_w_down, w_in, gla_w_lr, gla_b_lr, gla_norm, hgrn_lb_logits, hgrn_norm, mlstm_i_bias, mlstm_f_bias, mlstm_norm, w_out, ffn2_w_gate, ffn2_w_up, ffn2_w_down):
    bp, tp, _ = x_prompt.shape
    bs, ts, _ = x_sample.shape
    consts = _constants()
    yp = x_prompt.reshape(bp * tp, D_MODEL)
    ys = x_sample.reshape(bs * ts, D_MODEL)
    st_p = st_s = None
    mix = _mixer_weights(norm_gains, w_in, gla_w_lr, gla_b_lr, gla_norm, hgrn_lb_logits, hgrn_norm,
                         mlstm_i_bias, mlstm_f_bias, mlstm_norm, w_out)
    gains = mix["gains"]
    ffn1 = tuple(w.astype(BF16) for w in (ffn1_w_gate, ffn1_w_up, ffn1_w_down))
    ffn2 = tuple(w.astype(BF16) for w in (ffn2_w_gate, ffn2_w_up, ffn2_w_down))
    init = (state_gla, state_hgrn, state_mlstm_C, state_mlstm_n, state_mlstm_m)
    for l in range(DEPTH):
        yp, ys = _ffn(l, GAIN_FFN1, yp, ys, gains, *ffn1)
        yp3, st_p = _mixer(l, yp.reshape(bp, tp, D_MODEL), mix, consts, None, st_p, nb=bp, r=PROMPT_CHUNK)
        ys, st_s = _mixer(l, ys, mix, consts, init, st_s, nb=_sample_group(l), r=SAMPLE_ROWS)
        yp, ys = _ffn(l, GAIN_FFN2, yp3.reshape(bp * tp, D_MODEL), ys, gains, *ffn2)

    return (yp.reshape(bp, tp, D_MODEL), ys.reshape(bs, ts, D_MODEL)) + tuple(st_p) + tuple(st_s)
```

```python
import functools

import jax
import jax.numpy as jnp
import numpy as np
from jax import lax
from jax.experimental import pallas as pl
from jax.experimental.pallas import tpu as pltpu

F32 = jnp.float32
BF16 = jnp.bfloat16

D_MODEL = 1024
D_FF = 2688
DEPTH = 2
H = 4
DK_A, DV_A, RANK_A, TAU_A = 32, 64, 16, 16.0
E_B, DV_B = 64, 64
DK_C, DV_C = 128, 128
EPS = 1e-6
NEG_BIG = -1e30
M_INIT = -1e30
SPLIT_SIZES = (
    H * DK_A, H * DK_A, H * DV_A, H * DV_A, RANK_A,
    H * E_B, H * E_B, H * DV_B, H * DV_B,
    H * DK_C, H * DK_C, H * DV_C, H * DV_C, H, H,
)

LANES = 128
SUBLANES = 8
MXU_TILE = 256
VMEM_BYTES = 64 * 2**20
VMEM_TEMPORARIES = 16 * 2**20
GAIN_FFN1, GAIN_MIXER, GAIN_FFN2 = 0, 2, 4

SUB = SUBLANES
LA, VA = H * DK_A, H * DV_A
LB, VB = H * E_B, H * DV_B
N_FF_TILES = D_FF // MXU_TILE
TOKEN_TILE = 1024

C_AQ, C_AK, C_AV, C_AG = 0, LA, 2 * LA, 2 * LA + VA
C_BQ = C_AG + VA
C_BF, C_BI, C_BG = C_BQ + LB, C_BQ + 2 * LB, C_BQ + 3 * LB
C_CQ = C_BG + VB
C_CK, C_CV, C_CO = C_CQ + H * DK_C, C_CQ + 2 * H * DK_C, C_CQ + 2 * H * DK_C + H * DV_C
C_SMALL = C_CO + H * DV_C
C_FGATE = C_SMALL + LANES
N_IN_TILES = pl.cdiv(C_FGATE + LANES, MXU_TILE)
D_IN_PAD = N_IN_TILES * MXU_TILE
LANE_G = RANK_A

CHUNK_A, CHUNK_B = 64, 64
PAIR_GROUP = 8
FAST_DECAY_LIMIT = 60.0
LOG2_E = 1.4426950408889634


def _rms(x, g):
    return x * lax.rsqrt(jnp.mean(x * x, axis=-1, keepdims=True) + EPS) * g


def _sigmoid(x):
    return 1.0 / (1.0 + jnp.exp(-x))


def _silu(x):
    return x * _sigmoid(x)


def _log_sigmoid(x):
    return jnp.minimum(x, 0.0) - jnp.log1p(jnp.exp(-jnp.abs(x)))


def _dot(a, b):
    return jnp.dot(a, b, preferred_element_type=F32)


def _dot_tn(a, b):
    return lax.dot_general(a, b, (((0,), (0,)), ((), ())), preferred_element_type=F32)


def _dot_nt(a, b):
    return lax.dot_general(a, b, (((1,), (1,)), ((), ())), preferred_element_type=F32)


def _tri_cumsum(tri16, x):
    t = tri16.shape[0]
    hi = x.astype(BF16)
    r1 = x - hi.astype(F32)
    mid = r1.astype(BF16)
    lo = (r1 - mid.astype(F32)).astype(BF16)
    out = []
    for s in range(0, x.shape[0], t):
        out.append(_dot(tri16, hi[s:s + t]) + _dot(tri16, mid[s:s + t]) + _dot(tri16, lo[s:s + t]))
    return out[0] if len(out) == 1 else jnp.concatenate(out, axis=0)


def _pad_rows(x, mult):
    pad = -x.shape[0] % mult
    return x if pad == 0 else jnp.concatenate([x, jnp.zeros((pad, x.shape[1]), x.dtype)], axis=0)


def _ffn_kernel(gain_row, n_main, x_ref, xs_ref, gains_ref, wg_ref, wu_ref, wd_ref, o_ref, os_ref, h_scr, acc_scr):
    gpre_ref, gpost_ref = gains_ref.at[gain_row:gain_row + 1, :], gains_ref.at[gain_row + 1:gain_row + 2, :]

    def token_tile(xin_ref, out_ref):
        rows = xin_ref.shape[0]
        h_ref, acc_ref = h_scr.at[0:rows, :], acc_scr.at[0:rows, :]
        h_ref[...] = _rms(xin_ref[...], gpre_ref[...]).astype(BF16)

        def ff_cols(cols):
            h = h_ref[...]
            g = _dot(h, wg_ref[:, cols])
            u = _dot(h, wu_ref[:, cols])
            return _dot((_silu(g) * u).astype(BF16), wd_ref[cols, :])

        acc_ref[...] = ff_cols(slice(N_FF_TILES * MXU_TILE, D_FF))
        for t in range(N_FF_TILES):
            acc_ref[...] += ff_cols(slice(t * MXU_TILE, (t + 1) * MXU_TILE))
        out_ref[...] = xin_ref[...] + _rms(acc_ref[...], 0.5 * gpost_ref[...])

    step = pl.program_id(0)
    pl.when(step < n_main)(lambda: token_tile(x_ref, o_ref))
    pl.when(step == n_main)(lambda: token_tile(xs_ref, os_ref))


def _const_spec(shape):
    return pl.BlockSpec(shape, lambda *_: (0,) * len(shape), pipeline_mode=pl.Buffered(1))


def _layer_spec(layer, shape):
    return pl.BlockSpec((None,) + tuple(shape), lambda *_: (layer,) + (0,) * len(shape),
                        pipeline_mode=pl.Buffered(1))


def _ffn(layer, gain_row, x, xs, gains, wg, wu, wd):
    n, ns = x.shape[0], xs.shape[0]
    tm = TOKEN_TILE
    n_main = n // tm
    assert n_main * tm == n and ns <= tm
    main_map = lambda i: (jnp.minimum(i, n_main - 1), 0)
    weights = 3 * D_MODEL * D_FF * 2
    tiles = (tm + ns) * D_MODEL * 2 * 2 * 4 + tm * D_MODEL * (2 + 4) + 4 * tm * MXU_TILE * 4
    return pl.pallas_call(
        functools.partial(_ffn_kernel, gain_row, n_main),
        name="swiglu_half_step",
        out_shape=[jax.ShapeDtypeStruct((n, D_MODEL), F32), jax.ShapeDtypeStruct((ns, D_MODEL), F32)],
        grid=(n_main + 1,),
        in_specs=[
            pl.BlockSpec((tm, D_MODEL), main_map),
            pl.BlockSpec((ns, D_MODEL), lambda i: (0, 0)),
            _layer_spec(layer, gains.shape[1:]),
            _layer_spec(layer, (D_MODEL, D_FF)),
            _layer_spec(layer, (D_MODEL, D_FF)),
            _layer_spec(layer, (D_FF, D_MODEL)),
        ],
        out_specs=[pl.BlockSpec((tm, D_MODEL), main_map), pl.BlockSpec((ns, D_MODEL), lambda i: (0, 0))],
        scratch_shapes=[pltpu.VMEM((tm, D_MODEL), BF16), pltpu.VMEM((tm, D_MODEL), F32)],
        compiler_params=pltpu.CompilerParams(
            dimension_semantics=("arbitrary",),
            vmem_limit_bytes=min(VMEM_BYTES, weights + tiles + VMEM_TEMPORARIES),
        ),
    )(x, xs, gains, wg, wu, wd)


def _gated_linear_exact(q, k, v, la, s_prev, pat32, pat16, tri_sub16):
    r, l = q.shape
    vv = v.shape[1]
    n_sub = r // SUB
    b = _tri_cumsum(tri_sub16, la)
    b3, q3, k3, v3 = (a.reshape(n_sub, SUB, a.shape[1]) for a in (b, q, k, v))
    bend3 = b3[:, SUB - 1:SUB, :]
    qh3 = q3 * jnp.exp(b3)
    kh3 = k3 * jnp.exp(bend3 - b3)

    tpos = lax.broadcasted_iota(jnp.int32, (n_sub, SUB, l), 1)
    intra3 = jnp.zeros((n_sub, SUB, vv), F32)
    for j in range(SUB):
        arg = jnp.where(tpos >= j, b3 - b3[:, j:j + 1, :], NEG_BIG)
        p = (q3 * k3[:, j:j + 1, :] * jnp.exp(arg)).reshape(r, l)
        sc = _dot(p.astype(BF16), pat16)
        intra3 = intra3 + sc.reshape(n_sub, SUB, vv) * v3[:, j:j + 1, :]

    dcols = jnp.exp(_pad_rows(bend3.reshape(n_sub, l), SUBLANES)).T
    s = s_prev
    inter = []
    for i in range(n_sub):
        inter.append(_dot(qh3[i].astype(BF16), s.astype(BF16)))
        u = _dot_tn(kh3[i].astype(BF16), v3[i].astype(BF16))
        s = s * dcols[:, i:i + 1] + u * pat32
    o = inter[0] if n_sub == 1 else jnp.concatenate(inter, axis=0)
    return o + intra3.reshape(r, vv), s


def _gated_linear_fast(q, k, v, bc, s_scr, o_store, pat32, nb, r, c):
    l, vv = q.shape[1], v.shape[1]
    n_ch = r // c
    n_tot = nb * n_ch
    bc3, q3, k3, v3 = (a.reshape(n_tot, c, a.shape[1]) for a in (bc, q, k, v))
    bc3 = bc3 * LOG2_E
    bmid = bc3[:, c // 2 - 1:c // 2, :]
    blast = bc3[:, c - 1:c, :]
    qt = (q3 * jnp.exp2(bc3 - bmid)).astype(BF16)
    kt = (k3 * jnp.exp2(bmid - bc3)).astype(BF16)
    qc = (q3 * jnp.exp2(bc3)).astype(BF16)
    ke = (k3 * jnp.exp2(blast - bc3)).astype(BF16)
    v16 = v3.astype(BF16)
    dcols = jnp.exp2(_pad_rows(blast.reshape(n_tot, l), SUBLANES)).T

    head_l = lax.broadcasted_iota(jnp.int32, (c, l), 1) // (l // H)
    head_v = lax.broadcasted_iota(jnp.int32, (c, vv), 1) // (vv // H)
    causal = (lax.broadcasted_iota(jnp.int32, (H * c, c), 0) % c >= lax.broadcasted_iota(jnp.int32, (H * c, c), 1))
    zero16 = jnp.zeros((), BF16)

    idxs = range(n_tot)
    lhs = [jnp.concatenate([jnp.where(head_l == h, qt[i], zero16) for h in range(H)], axis=0) for i in idxs]
    sc = [_dot_nt(lhs[i], kt[i]) for i in idxs]
    sc = [jnp.where(causal, sc[i], 0.0).astype(BF16) for i in idxs]
    ov = [_dot(sc[i], v16[i]) for i in idxs]
    intra = [functools.reduce(jnp.add, [jnp.where(head_v == h, ov[i][h * c:(h + 1) * c], 0.0) for h in range(H)])
             for i in idxs]
    u = [_dot_tn(ke[i], v16[i]) for i in idxs]

    for ci in range(n_ch):
        ids = [b * n_ch + ci for b in range(nb)]
        s_old = [s_scr[b] for b in range(nb)]
        inter = [_dot(qc[i], s_old[b].astype(BF16)) for b, i in enumerate(ids)]
        for b, i in enumerate(ids):
            o_store(b, ci * c, c, intra[i] + inter[b])
            s_scr[b] = s_old[b] * dcols[:, i:i + 1] + u[i] * pat32


def _expand_heads(per_head):
    dk, dv = per_head[0].shape
    rows = []
    for h in range(H):
        parts = [jnp.zeros((dk, h * dv), F32), per_head[h], jnp.zeros((dk, (H - 1 - h) * dv), F32)]
        rows.append(jnp.concatenate([p for p in parts if p.shape[1]], axis=1))
    return jnp.concatenate(rows, axis=0)


def _head_block(s, h):
    dk, dv = s.shape[0] // H, s.shape[1] // H
    return s[h * dk:(h + 1) * dk, h * dv:(h + 1) * dv]


def _mixer_kernel(layer, nb, r, valid, has_init, n_steps, *refs):
    (x_ref, gains_ref, win_ref, wlr_ref, blr_ref, gnorm_ref, lb_ref, hnorm_ref,
     ibias_ref, fbias_ref, mnorm_ref, wout_ref, pa32_ref, pa16_ref, pb32_ref, pb16_ref, trisub_ref,
     tria_ref, trib_ref, tric_ref) = refs[:20]
    refs = refs[20:]
    gpre_ref = gains_ref.at[GAIN_MIXER:GAIN_MIXER + 1, :]
    gpost_ref = gains_ref.at[GAIN_MIXER + 1:GAIN_MIXER + 2, :]
    if has_init:
        gla0_ref, hgrn0_ref, c0_ref, n0_ref, m0_ref = refs[:5]
        refs = refs[5:]
    if layer:
        earlier_refs = refs[:5]
        refs = refs[5:]
    y_ref = refs[0]
    out_refs = refs[1:6]
    gla_ref, hgrn_ref, c_ref, n_ref, m_ref = (o.at[layer] for o in out_refs)
    hin_scr, proj_scr, mixed_scr, sgla_scr, shgrn_scr, cst_scr, nrow_scr, mrep_scr = refs[6:]
    n_as_column = not has_init
    rows_all = nb * r
    ca, cb = min(CHUNK_A, r), min(CHUNK_B, r)
    step = pl.program_id(0)

    if has_init:
        for b in range(nb):
            sgla_scr[b] = _expand_heads([gla0_ref[b, h] for h in range(H)])
            shgrn_scr[b] = _expand_heads([hgrn0_ref[b, h] for h in range(H)])
            for h in range(H):
                mrep_scr[b, h:h + 1, :] = jnp.broadcast_to(m0_ref[b:b + 1, h:h + 1], (1, LANES))
        cst_scr[...] = c0_ref[...]
        nrow_scr[...] = n0_ref[...]
    else:
        @pl.when(step == 0)
        def _():
            sgla_scr[...] = jnp.zeros_like(sgla_scr)
            shgrn_scr[...] = jnp.zeros_like(shgrn_scr)
            cst_scr[...] = jnp.zeros_like(cst_scr)
            mrep_scr[...] = jnp.full(mrep_scr.shape, M_INIT, F32)

    def load_x():
        if valid == r:
            return x_ref[...].reshape(rows_all, D_MODEL)
        x_in = x_ref[...]
        pad = jnp.zeros((r - valid, D_MODEL), F32)
        return jnp.concatenate([p for b in range(nb) for p in (x_in[b * valid:(b + 1) * valid], pad)], axis=0)

    hin_scr[...] = _rms(load_x(), gpre_ref[...]).astype(BF16)

    for t in range(N_IN_TILES):
        cols = slice(t * MXU_TILE, (t + 1) * MXU_TILE)
        proj_scr[:, cols] = _dot(hin_scr[...], win_ref[:, cols])

    lg = [lb_ref[j:j + 1, :] for j in range(DEPTH)]
    lg_max = functools.reduce(jnp.maximum, lg)
    ex = [jnp.exp(v - lg_max) for v in lg]
    ex_sum = functools.reduce(jnp.add, ex)
    lb = functools.reduce(jnp.add, [e / ex_sum for e in ex[1:layer + 1]], jnp.zeros_like(lg_max))

    if valid < r:
        row_ok_all = lax.broadcasted_iota(jnp.int32, (rows_all, 1), 0) % r < valid
        row_ok = row_ok_all[0:r]

    def store_cols(c0):
        def store(b, t0, n, val):
            mixed_scr[b * r + t0:b * r + t0 + n, c0:c0 + val.shape[1]] = val
        return store

    def exact_path(q_of, k_of, v_of, la_of, s_scr, pat32_ref, pat16_ref, c0):
        def one(bi, carry):
            rows = pl.ds(pl.multiple_of(bi * r, r), r)
            o, s1 = _gated_linear_exact(q_of(rows), k_of(rows), v_of(rows), la_of(rows), s_scr[bi],
                                        pat32_ref[...], pat16_ref[...], trisub_ref[...])
            mixed_scr[rows, c0:c0 + o.shape[1]] = o
            s_scr[bi] = s1
            return carry
        lax.fori_loop(0, nb, one, 0)

    def gated_linear(q_of, k_of, v_of, la_of, tri_ref, s_scr, pat32_ref, pat16_ref, c0, c):
        everything = slice(None)
        k_all = k_of(everything)
        bc = _tri_cumsum(tri_ref[...], la_of(everything, k_all))
        bc3 = bc.reshape(rows_all // c, c, bc.shape[1])
        to_mid, past_mid = -bc3[:, c // 2 - 1, :], bc3[:, c // 2 - 1, :] - bc3[:, c - 1, :]
        safe = jnp.max(jnp.maximum(to_mid, past_mid)) <= FAST_DECAY_LIMIT

        @pl.when(safe)
        def _():
            _gated_linear_fast(q_of(everything), k_all, v_of(everything), bc, s_scr,
                               store_cols(c0), pat32_ref[...], nb, r, c)

        @pl.when(jnp.logical_not(safe))
        def _():
            exact_path(q_of, k_of, v_of, la_of, s_scr, pat32_ref, pat16_ref, c0)

    def la_a(rows, k=None):
        small = proj_scr[rows, C_SMALL:C_SMALL + LANES]
        la = _log_sigmoid(_dot(small.astype(BF16), wlr_ref[...]) + blr_ref[...]) * (1.0 / TAU_A)
        if valid < r:
            la = jnp.where(row_ok_all if rows == slice(None) else row_ok, la, 0.0)
        return la

    gated_linear(lambda rows: proj_scr[rows, C_AQ:C_AQ + LA] * DK_A ** -0.5,
                 lambda rows: proj_scr[rows, C_AK:C_AK + LA],
                 lambda rows: proj_scr[rows, C_AV:C_AV + VA],
                 la_a, tria_ref, sgla_scr, pa32_ref, pa16_ref, 0, ca)

    def k_b(rows):
        kb = (1.0 - lb) * _sigmoid(-proj_scr[rows, C_BF:C_BF + LB])
        if valid < r:
            kb = jnp.where(row_ok_all if rows == slice(None) else row_ok, kb, 0.0)
        return kb

    gated_linear(lambda rows: _silu(proj_scr[rows, C_BQ:C_BQ + LB]), k_b,
                 lambda rows: proj_scr[rows, C_BI:C_BI + VB],
                 lambda rows, k=None: jnp.log1p(-(k_b(rows) if k is None else k)),
                 trib_ref, shgrn_scr, pb32_ref, pb16_ref, VA, cb)

    i_pre = proj_scr[:, C_SMALL:C_SMALL + LANES] + ibias_ref[...]
    lsf = _log_sigmoid(proj_scr[:, C_FGATE:C_FGATE + LANES] + fbias_ref[...])
    if valid < r:
        lsf = jnp.where(row_ok_all, lsf, 0.0)
        i_pre = jnp.where(row_ok_all, i_pre, NEG_BIG)
    f_cum = _tri_cumsum(tric_ref[...], lsf)
    a_all = i_pre - f_cum
    causal = (lax.broadcasted_iota(jnp.int32, (r, r), 0) >= lax.broadcasted_iota(jnp.int32, (r, r), 1))
    ones16 = jnp.ones((r, DV_C), BF16)

    def lanes(col):
        return jnp.broadcast_to(col, (col.shape[0], LANES))

    def head_cols(c0, b, h):
        return proj_scr[b * r:(b + 1) * r, c0 + h * DK_C:c0 + (h + 1) * DK_C]

    for b0 in range(0, nb, PAIR_GROUP // H):
        pairs = [(b, h) for b in range(b0, b0 + PAIR_GROUP // H) for h in range(H)]
        a_rows = {b: a_all[b * r:(b + 1) * r].T for b in range(b0, b0 + PAIR_GROUP // H)}
        f_rep = {(b, h): lanes(f_cum[b * r:(b + 1) * r, LANE_G + h:LANE_G + h + 1]) for b, h in pairs}
        a_rep = {(b, h): lanes(a_all[b * r:(b + 1) * r, LANE_G + h:LANE_G + h + 1]) for b, h in pairs}
        m_prev = {(b, h): mrep_scr[b, h:h + 1, :] for b, h in pairs}
        a_mat = {(b, h): jnp.where(causal, a_rows[b][LANE_G + h:LANE_G + h + 1, :], NEG_BIG) for b, h in pairs}
        g = {p: jnp.maximum(lanes(jnp.max(a_mat[p], axis=1, keepdims=True)), m_prev[p]) for p in pairs}
        d = {p: jnp.exp(a_mat[p] - g[p][:, 0:r]) for p in pairs}
        w_prev = {p: jnp.exp(m_prev[p] - g[p]) for p in pairs}
        m_t = {p: f_rep[p] + g[p] for p in pairs}
        q = {(b, h): head_cols(C_CQ, b, h) for b, h in pairs}
        q16 = {p: q[p].astype(BF16) for p in pairs}
        k = {(b, h): head_cols(C_CK, b, h) * DK_C ** -0.5 for b, h in pairs}
        v16 = {(b, h): head_cols(C_CV, b, h).astype(BF16) for b, h in pairs}
        if n_as_column:
            v16 = {p: jnp.concatenate([v16[p], ones16], axis=1) for p in pairs}
        qk = {p: _dot_nt(q16[p], k[p].astype(BF16)) * d[p] for p in pairs}
        c_prev = {(b, h): cst_scr[b, h] for b, h in pairs}
        now = {p: _dot(qk[p].astype(BF16), v16[p]) for p in pairs}
        old = {p: _dot(q16[p], c_prev[p].astype(BF16)) for p in pairs}
        for b, h in pairs:
            p = (b, h)
            if n_as_column:
                both = now[p] + jnp.concatenate([w_prev[p], w_prev[p]], axis=1) * old[p]
                num, den = both[:, 0:DV_C], both[:, DV_C:]
            else:
                num = now[p] + w_prev[p] * old[p]
                den = lanes(jnp.sum(qk[p], axis=1, keepdims=True)) + w_prev[p] * lanes(
                    jnp.sum(q[p] * nrow_scr[b, h:h + 1, :], axis=1, keepdims=True))
            mixed_scr[b * r:(b + 1) * r, VA + VB + h * DV_C:VA + VB + (h + 1) * DV_C] = (
                num / jnp.maximum(jnp.abs(den), jnp.exp(-m_t[p])))
        g_last = {p: g[p][r - 1:r, :] for p in pairs}
        w_old = {p: jnp.exp(m_prev[p] - g_last[p]) for p in pairs}
        kw = {p: k[p] * jnp.exp(a_rep[p] - g_last[p]) for p in pairs}
        upd = {p: _dot_tn(kw[p].astype(BF16), v16[p]) for p in pairs}
        for b, h in pairs:
            p = (b, h)
            if n_as_column:
                cst_scr[b, h] = jnp.concatenate([w_old[p], w_old[p]], axis=1) * c_prev[p] + upd[p]
            else:
                cst_scr[b, h] = w_old[p] * c_prev[p] + upd[p]
                nrow_scr[b, h:h + 1, :] = (w_old[p] * nrow_scr[b, h:h + 1, :]
                                           + jnp.sum(kw[p], axis=0, keepdims=True))
            mrep_scr[b, h:h + 1, :] = m_t[p][r - 1:r, :]

    def fold_states():
        for b in range(nb):
            for h in range(H):
                gla_ref[b, h] = _head_block(sgla_scr[b], h)
                hgrn_ref[b, h] = _head_block(shgrn_scr[b], h)
                c_ref[b, h] = cst_scr[b, h, :, 0:DV_C]
                if n_as_column:
                    n_ref[b, h:h + 1, :] = cst_scr[b, h, :, DV_C:].T[0:1, :]
                m_ref[b:b + 1, h:h + 1] = mrep_scr[b, h:h + 1, 0:1]
            if not n_as_column:
                n_ref[b] = nrow_scr[b]
        if layer:
            for prev, out in zip(earlier_refs, out_refs):
                out[0:layer] = prev[...]

    if has_init:
        fold_states()
    else:
        pl.when(step == n_steps - 1)(fold_states)

    o_a = mixed_scr[:, 0:VA]
    ms = _dot((o_a * o_a).astype(BF16), pb16_ref[...]) * (1.0 / DV_A)
    mixed_scr[:, 0:VA] = o_a * lax.rsqrt(ms + EPS) * gnorm_ref[...] * _silu(proj_scr[:, C_AG:C_AG + VA])
    o_b = mixed_scr[:, VA:VA + VB]
    ms = _dot((o_b * o_b).astype(BF16), pb16_ref[...]) * (1.0 / DV_B)
    mixed_scr[:, VA:VA + VB] = o_b * lax.rsqrt(ms + EPS) * hnorm_ref[...] * _silu(proj_scr[:, C_BG:C_BG + VB])
    for h in range(H):
        cols = slice(VA + VB + h * DV_C, VA + VB + (h + 1) * DV_C)
        gate = _sigmoid(proj_scr[:, C_CO + h * DV_C:C_CO + (h + 1) * DV_C])
        mixed_scr[:, cols] = gate * _rms(mixed_scr[:, cols], mnorm_ref[...])
    y = _dot(mixed_scr[...].astype(BF16), wout_ref[...])
    y = load_x() + _rms(y, gpost_ref[...])
    if valid == r:
        y_ref[...] = y.reshape(nb, r, D_MODEL)
    else:
        y_ref[...] = jnp.concatenate([y[b * r:b * r + valid] for b in range(nb)], axis=0)


def _block_tri(n, block):
    i, j = np.arange(n)[:, None], np.arange(n)[None, :]
    return jnp.asarray((j <= i) & (i // block == j // block), BF16)


def _mixer(layer, x, w, consts, init=None, earlier=None, *, nb, r):
    has_init = init is not None
    if has_init:
        batch = init[0].shape[1]
        valid = x.shape[0] // batch
        n_steps = batch // nb
        x_spec = pl.BlockSpec((nb * valid, D_MODEL), lambda i: (i, 0))
        b_idx = lambda i: i
    else:
        batch, tokens, _ = x.shape
        assert batch == nb
        valid = r
        n_steps = tokens // r
        x_spec = pl.BlockSpec((nb, r, D_MODEL), lambda i: (0, i, 0))
        b_idx = lambda i: 0
    rows_all = nb * r
    slab = min(MXU_TILE, rows_all)
    tris = [_block_tri(r, SUB), _block_tri(slab, min(CHUNK_A, r)), _block_tri(slab, min(CHUNK_B, r)),
            _block_tri(slab, r)]

    state_shapes = [(batch, H, DK_A, DV_A), (batch, H, E_B, DV_B), (batch, H, DK_C, DV_C), (batch, H, DK_C),
                    (batch, H)]

    def stacked_specs(n_layers):
        return [pl.BlockSpec((n_layers, nb) + s[1:],
                             (lambda nd: (lambda i: (0, b_idx(i)) + (0,) * (nd - 1)))(len(s))) for s in state_shapes]
    per_layer = [w["gains"], w["w_in"], w["w_lr"], w["b_lr"], w["gla_norm"]]
    per_layer2 = [w["hgrn_norm"], w["i_bias"], w["f_bias"], w["mlstm_norm"], w["w_out"]]
    shared = [consts["pa32"], consts["pa16"], consts["pb32"], consts["pb16"]] + tris
    in_arrays = [x] + per_layer + [w["lb_logits"]] + per_layer2 + shared
    in_specs = ([x_spec] + [_layer_spec(layer, a.shape[1:]) for a in per_layer]
                + [_const_spec(w["lb_logits"].shape)] + [_layer_spec(layer, a.shape[1:]) for a in per_layer2]
                + [_const_spec(a.shape) for a in shared])
    if has_init:
        in_arrays += list(init)
        in_specs += [pl.BlockSpec((None, nb) + s[1:], (lambda nd: (lambda i: (layer, i) + (0,) * (nd - 1)))(len(s)))
                     for s in state_shapes]
    if layer:
        in_arrays += list(earlier)
        in_specs += stacked_specs(layer)
    scratch = [
        pltpu.VMEM((rows_all, D_MODEL), BF16),
        pltpu.VMEM((rows_all, D_IN_PAD), F32),
        pltpu.VMEM((rows_all, D_MODEL), F32),
        pltpu.VMEM((nb, LA, VA), F32),
        pltpu.VMEM((nb, LB, VB), F32),
        pltpu.VMEM((nb, H, DK_C, DV_C if has_init else 2 * DV_C), F32),
        pltpu.VMEM((nb, H, DK_C), F32),
        pltpu.VMEM((nb, H, LANES), F32),
    ]
    state_block = nb * 4 * (LA * LANES + LB * LANES + H * DK_C * DV_C + SUBLANES * LANES + LANES)
    vmem = (2 * (D_MODEL * D_IN_PAD + D_MODEL * D_MODEL)
            + rows_all * (D_MODEL * (2 + 4 + 4 * 4) + D_IN_PAD * 4)
            + nb * 4 * (LA * VA + LB * VB + H * DK_C * 2 * DV_C + 2 * SUBLANES * LANES)
            + state_block * ((2 if has_init else 0) + 2 * layer + 2 * (layer + 1)))
    outs = pl.pallas_call(
        functools.partial(_mixer_kernel, layer, nb, r, valid, has_init, n_steps),
        name="token_mixer_sample" if has_init else "token_mixer_prompt",
        out_shape=([jax.ShapeDtypeStruct(x.shape, F32)]
                   + [jax.ShapeDtypeStruct((layer + 1,) + s, F32) for s in state_shapes]),
        grid=(n_steps,),
        in_specs=in_specs,
        out_specs=[x_spec] + stacked_specs(layer + 1),
        scratch_shapes=scratch,
        compiler_params=pltpu.CompilerParams(
            dimension_semantics=("arbitrary",),
            vmem_limit_bytes=min(VMEM_BYTES, vmem + VMEM_TEMPORARIES),
        ),
    )(*in_arrays)
    return outs[0], outs[1:]


def _head_pattern(rows_per_head, cols_per_head):
    rr = np.arange(H * rows_per_head)[:, None] // rows_per_head
    cc = np.arange(H * cols_per_head)[None, :] // cols_per_head
    return (rr == cc).astype(np.float32)


def _constants():
    pa, pb = _head_pattern(DK_A, DV_A), _head_pattern(E_B, DV_B)
    return {"pa32": jnp.asarray(pa), "pa16": jnp.asarray(pa, BF16),
            "pb32": jnp.asarray(pb), "pb16": jnp.asarray(pb, BF16)}


def _mixer_weights(norm_gains, w_in, gla_w_lr, gla_b_lr, gla_norm, hgrn_lb_logits, hgrn_norm,
                   mlstm_i_bias, mlstm_f_bias, mlstm_norm, w_out):
    (a_q, a_k, a_v, a_g, a_lr, b_q, b_f, b_i, b_g, c_q, c_k, c_v, c_o, c_i, c_f) = jnp.split(
        w_in, np.cumsum(SPLIT_SIZES)[:-1].tolist(), axis=2)
    zeros = lambda n: jnp.zeros((DEPTH, D_MODEL, n), F32)
    packed = jnp.concatenate(
        [a_q, a_k, a_v, a_g, b_q, b_f, b_i, b_g, c_q, c_k, c_v, c_o,
         a_lr, c_i, zeros(LANES - RANK_A - H), zeros(LANE_G), c_f, zeros(D_IN_PAD - C_FGATE - LANE_G - H)], axis=2)
    rows = lambda v: v.reshape(DEPTH, 1, -1).astype(F32)
    gate_lanes = lambda v: rows(jnp.pad(v, ((0, 0), (LANE_G, LANES - LANE_G - H))))
    return {
        "gains": norm_gains.astype(F32),
        "w_in": packed.astype(BF16),
        "w_lr": jnp.pad(gla_w_lr, ((0, 0), (0, LANES - RANK_A), (0, 0))).astype(BF16),
        "b_lr": rows(gla_b_lr),
        "gla_norm": rows(jnp.tile(gla_norm, (1, H))),
        "lb_logits": hgrn_lb_logits.astype(F32),
        "hgrn_norm": rows(jnp.tile(hgrn_norm, (1, H))),
        "i_bias": gate_lanes(mlstm_i_bias), "f_bias": gate_lanes(mlstm_f_bias),
        "mlstm_norm": rows(mlstm_norm),
        "w_out": w_out.astype(BF16),
    }


PROMPT_CHUNK = 64
SAMPLE_ROWS = SUBLANES
STATE_WINDOW_BUDGET = 32 * 2**20


def _sample_group(layer):
    per_sequence = 4 * (LA * LANES + LB * LANES + H * DK_C * DV_C + SUBLANES * LANES + LANES)
    windows = 2 * (1 + layer + layer + 1)
    nb = SUBLANES
    while 2 * nb * per_sequence * windows <= STATE_WINDOW_BUDGET:
        nb *= 2
    return nb


def kernel(x_prompt, x_sample, state_gla, state_hgrn, state_mlstm_C, state_mlstm_n, state_mlstm_m, norm_gains, ffn1_w_gate, ffn1_w_up, ffn1_w_down, w_in, gla_w_lr, gla_b_lr, gla_norm, hgrn_lb_logits, hgrn_norm, mlstm_i_bias, mlstm_f_bias, mlstm_norm, w_out, ffn2_w_gate, ffn2_w_up, ffn2_w_down):
    bp, tp, _ = x_prompt.shape
    bs, ts, _ = x_sample.shape
    consts = _constants()
    yp = x_prompt.reshape(bp * tp, D_MODEL)
    ys = x_sample.reshape(bs * ts, D_MODEL)
    st_p = st_s = None
    mix = _mixer_weights(norm_gains, w_in, gla_w_lr, gla_b_lr, gla_norm, hgrn_lb_logits, hgrn_norm,
                         mlstm_i_bias, mlstm_f_bias, mlstm_norm, w_out)
    gains = mix["gains"]
    ffn1 = tuple(w.astype(BF16) for w in (ffn1_w_gate, ffn1_w_up, ffn1_w_down))
    ffn2 = tuple(w.astype(BF16) for w in (ffn2_w_gate, ffn2_w_up, ffn2_w_down))
    init = (state_gla, state_hgrn, state_mlstm_C, state_mlstm_n, state_mlstm_m)
    for l in range(DEPTH):
        yp, ys = _ffn(l, GAIN_FFN1, yp, ys, gains, *ffn1)
        yp3, st_p = _mixer(l, yp.reshape(bp, tp, D_MODEL), mix, consts, None, st_p, nb=bp, r=PROMPT_CHUNK)
        ys, st_s = _mixer(l, ys, mix, consts, init, st_s, nb=_sample_group(l), r=SAMPLE_ROWS)
        yp, ys = _ffn(l, GAIN_FFN2, yp3.reshape(bp * tp, D_MODEL), ys, gains, *ffn2)

    return (yp.reshape(bp, tp, D_MODEL), ys.reshape(bs, ts, D_MODEL)) + tuple(st_p) + tuple(st_s)
```
